```python
import jax, jax.numpy as jnp
from jax import lax
import numpy as np

D_MODEL = 2048
BATCH = 2
SEQ = 4096
DEPTH = 2

GRID_W = 64
CTX_LEN = 256
N_MIXERS = 2
N_ADA = 6
EPS = 1e-6
CONV_K = 31
CONV_PAD = CONV_K // 2
LRU_HEADS = 16
LRU_BLOCK = D_MODEL // LRU_HEADS
LRU_C = 8.0
LRU_CONV_K = 4
LRU_CONV_PAD = (1, 2)
N_EXPERTS = 32
TOP_K = 4
D_FF = D_MODEL // 2
SWIGLU_ALPHA = 1.702
SWIGLU_LIMIT = 7.0
N_CONV_LAYERS = (DEPTH + 1) // 2
N_LRU_LAYERS = DEPTH // 2

kernel_name = "hybrid_conformer_rglru_moe_dit"


def rmsnorm(x, g):
    xf = x.astype(jnp.float32)
    y = xf * lax.rsqrt(jnp.mean(xf * xf, axis=-1, keepdims=True) + EPS)
    return (y * g.astype(jnp.float32)).astype(x.dtype)


def layernorm(x, g, b):
    xf = x.astype(jnp.float32)
    mu = jnp.mean(xf, axis=-1, keepdims=True)
    xc = xf - mu
    y = xc * lax.rsqrt(jnp.mean(xc * xc, axis=-1, keepdims=True) + EPS)
    return (y * g.astype(jnp.float32) + b.astype(jnp.float32)).astype(x.dtype)


def modulate(xn, shift, scale):
    return xn * (1.0 + scale) + shift


def depthwise_conv1d(x, w, pad):
    ch = x.shape[-1]
    return lax.conv_general_dilated(
        x, w[:, None, :].astype(x.dtype), window_strides=(1,), padding=[pad],
        dimension_numbers=("NWC", "WIO", "NWC"), feature_group_count=ch)


def grid_axial_depthwise_conv(x, w):
    bsz, n, d = x.shape
    rows = n // GRID_W
    half = d // 2
    g = x.reshape(bsz, rows, GRID_W, d)
    gh = g[..., :half].reshape(bsz * rows, GRID_W, half)
    yh = depthwise_conv1d(gh, w[:, :half], (CONV_PAD, CONV_PAD)).reshape(bsz, rows, GRID_W, half)
    gv = jnp.swapaxes(g[..., half:], 1, 2).reshape(bsz * GRID_W, rows, half)
    yv = depthwise_conv1d(gv, w[:, half:], (CONV_PAD, CONV_PAD)).reshape(bsz, GRID_W, rows, half)
    yv = jnp.swapaxes(yv, 1, 2)
    return jnp.concatenate([yh, yv], axis=-1).reshape(bsz, n, d)


def conformer_conv_module(xm, on_grid, pw1_w, pw1_b, dw_w, dw_b, ln_g, ln_b, pw2_w, pw2_b):
    u = xm @ pw1_w + pw1_b
    v = u[..., :D_MODEL] * jax.nn.sigmoid(u[..., D_MODEL:])
    if on_grid:
        v = grid_axial_depthwise_conv(v, dw_w)
    else:
        v = depthwise_conv1d(v, dw_w, (CONV_PAD, CONV_PAD))
    v = layernorm(v + dw_b, ln_g, ln_b)
    v = jax.nn.silu(v)
    return v @ pw2_w + pw2_b


def lru_gates(xb, wa, ba, wi, bi, lam):
    bsz, n, d = xb.shape
    xh = xb.reshape(bsz, n, LRU_HEADS, LRU_BLOCK)
    r = jax.nn.sigmoid(jnp.einsum("bnhi,hij->bnhj", xh, wa).reshape(bsz, n, d) + ba)
    i = jax.nn.sigmoid(jnp.einsum("bnhi,hij->bnhj", xh, wi).reshape(bsz, n, d) + bi)
    log_a = -LRU_C * r.astype(jnp.float32) * jax.nn.softplus(-lam.astype(jnp.float32))
    a = jnp.exp(log_a)
    mult = jnp.sqrt(-jnp.expm1(2.0 * log_a))
    u = xb.astype(jnp.float32) * i.astype(jnp.float32) * mult
    return a, u


def linear_scan(a, u, h0):
    u = u.at[:, 0].add(a[:, 0] * h0)
    def combine(earlier, later):
        return (earlier[0] * later[0], later[0] * earlier[1] + later[1])
    _, h = lax.associative_scan(combine, (a, u), axis=1)
    return h


def rglru_block(xc, xl, ctx_out, wy, by, wx, bx, conv_w, conv_b, wa, ba, wi, bi, lam, wo, bo):
    dt = xl.dtype
    def conv_branch(xm):
        return depthwise_conv1d(xm @ wx + bx, conv_w, LRU_CONV_PAD) + conv_b
    xbc = conv_branch(xc)
    xbl = conv_branch(xl)
    bsz = xl.shape[0]
    h0 = jnp.zeros((bsz, D_MODEL), jnp.float32)
    rnn_c = 0.0
    rnn_l = 0.0
    for d in range(2):
        ac, uc = lru_gates(xbc, wa[d], ba[d], wi[d], bi[d], lam[d])
        al, ul = lru_gates(xbl, wa[d], ba[d], wi[d], bi[d], lam[d])
        if d == 1:
            ac, uc, al, ul = (jnp.flip(t, axis=1) for t in (ac, uc, al, ul))
        hc = linear_scan(ac, uc, h0)
        hl = linear_scan(al, ul, hc[:, -1])
        if d == 1:
            hc = jnp.flip(hc, axis=1)
            hl = jnp.flip(hl, axis=1)
        rnn_c = rnn_c + hc
        rnn_l = rnn_l + hl
    yl = jax.nn.gelu(xl @ wy + by)
    out_l = (yl * rnn_l.astype(dt)) @ wo + bo
    out_c = None
    if ctx_out:
        yc = jax.nn.gelu(xc @ wy + by)
        out_c = (yc * rnn_c.astype(dt)) @ wo + bo
    return out_l, out_c


def moe_ffn(h, router_w, router_b, w1, b1, w2, b2):
    t = h.reshape(-1, D_MODEL)
    logits = (t @ router_w + router_b).astype(jnp.float32)
    top_v, top_i = lax.top_k(logits, TOP_K)
    probs = jax.nn.softmax(top_v, axis=-1)
    gate = jnp.sum(jax.nn.one_hot(top_i, N_EXPERTS, dtype=jnp.float32) * probs[..., None], axis=1)

    def expert_step(acc, xs):
        w1e, b1e, w2e, b2e, ge = xs
        hh = t @ w1e + b1e
        glu = jnp.minimum(hh[:, :D_FF], SWIGLU_LIMIT)
        lin = jnp.clip(hh[:, D_FF:], -SWIGLU_LIMIT, SWIGLU_LIMIT)
        act = glu * jax.nn.sigmoid(SWIGLU_ALPHA * glu) * (lin + 1.0)
        y = act @ w2e + b2e
        return acc + ge[:, None].astype(acc.dtype) * y, None

    out, _ = lax.scan(expert_step, jnp.zeros_like(t), (w1, b1, w2, b2, gate.T))
    return out.reshape(h.shape)


def setup_inputs(seed: int = 0) -> dict:
    key = jax.random.key(seed)
    ks = iter(jax.random.split(key, 40))
    f32 = jnp.float32
    D, E, F = D_MODEL, N_EXPERTS, D_FF
    NA, NB = N_CONV_LAYERS, N_LRU_LAYERS
    nrm = lambda shape, s: jax.random.normal(next(ks), shape, f32) * s
    gain = lambda shape: 1.0 + nrm(shape, 0.01)
    u = jax.random.uniform(next(ks), (NB, 2, D), f32, 0.9, 0.999)
    sig = u ** (1.0 / LRU_C)
    lam = jnp.log(sig) - jnp.log1p(-sig)
    return {
        "x": nrm((BATCH, SEQ, D), 1.0),
        "c": nrm((BATCH, D), 1.0),
        "ctx": nrm((BATCH, CTX_LEN, D), 1.0),
        "c_ctx": nrm((D,), 1.0),
        "ada_w": nrm((DEPTH, D, N_ADA * D), 0.5 * D ** -0.5),
        "ada_b": nrm((DEPTH, N_ADA * D), 0.02),
        "norm1_g": gain((DEPTH, D)),
        "norm2_g": gain((DEPTH, D)),
        "conv_pw1_w": nrm((NA, D, 2 * D), D ** -0.5),
        "conv_pw1_b": nrm((NA, 2 * D), 0.01),
        "conv_dw_w": nrm((NA, CONV_K, D), CONV_K ** -0.5),
        "conv_dw_b": nrm((NA, D), 0.01),
        "conv_ln_g": gain((NA, D)),
        "conv_ln_b": nrm((NA, D), 0.01),
        "conv_pw2_w": nrm((NA, D, D), D ** -0.5),
        "conv_pw2_b": nrm((NA, D), 0.01),
        "lru_wy": nrm((NB, D, D), D ** -0.5),
        "lru_by": nrm((NB, D), 0.01),
        "lru_wx": nrm((NB, D, D), D ** -0.5),
        "lru_bx": nrm((NB, D), 0.01),
        "lru_conv_w": nrm((NB, LRU_CONV_K, D), LRU_CONV_K ** -0.5),
        "lru_conv_b": nrm((NB, D), 0.01),
        "lru_wa": nrm((NB, 2, LRU_HEADS, LRU_BLOCK, LRU_BLOCK), LRU_BLOCK ** -0.5),
        "lru_ba": nrm((NB, 2, D), 0.01),
        "lru_wi": nrm((NB, 2, LRU_HEADS, LRU_BLOCK, LRU_BLOCK), LRU_BLOCK ** -0.5),
        "lru_bi": nrm((NB, 2, D), 0.01),
        "lru_lambda": lam,
        "lru_wo": nrm((NB, D, D), D ** -0.5),
        "lru_bo": nrm((NB, D), 0.01),
        "router_w": nrm((DEPTH, D, E), D ** -0.5),
        "router_b": nrm((DEPTH, E), 0.01),
        "moe_w1": nrm((DEPTH, E, D, 2 * F), D ** -0.5),
        "moe_b1": nrm((DEPTH, E, 2 * F), 0.01),
        "moe_w2": nrm((DEPTH, E, F, D), F ** -0.5),
        "moe_b2": nrm((DEPTH, E, D), 0.01),
        "final_g": gain((D,)),
    }


def reference(x, c, ctx, c_ctx, ada_w, ada_b, norm1_g, norm2_g,
              conv_pw1_w, conv_pw1_b, conv_dw_w, conv_dw_b, conv_ln_g, conv_ln_b, conv_pw2_w, conv_pw2_b,
              lru_wy, lru_by, lru_wx, lru_bx, lru_conv_w, lru_conv_b, lru_wa, lru_ba, lru_wi, lru_bi,
              lru_lambda, lru_wo, lru_bo,
              router_w, router_b, moe_w1, moe_b1, moe_w2, moe_b2, final_g):
    h_lat = x
    h_ctx = ctx
    src_lat = jax.nn.silu(c)
    src_ctx = jax.nn.silu(c_ctx)[None]
    ia = 0
    ib = 0
    for layer in range(DEPTH):
        last = layer == DEPTH - 1
        m_lat = (src_lat @ ada_w[layer] + ada_b[layer])[:, None, :]
        m_ctx = (src_ctx @ ada_w[layer] + ada_b[layer])[:, None, :]
        sh1_l, sc1_l, g1_l, sh2_l, sc2_l, g2_l = jnp.split(m_lat, N_ADA, axis=-1)
        sh1_c, sc1_c, g1_c, sh2_c, sc2_c, g2_c = jnp.split(m_ctx, N_ADA, axis=-1)

        xn_l = modulate(rmsnorm(h_lat, norm1_g[layer]), sh1_l, sc1_l)
        xn_c = modulate(rmsnorm(h_ctx, norm1_g[layer]), sh1_c, sc1_c)
        if layer % N_MIXERS == 0:
            pa = (conv_pw1_w[ia], conv_pw1_b[ia], conv_dw_w[ia], conv_dw_b[ia],
                  conv_ln_g[ia], conv_ln_b[ia], conv_pw2_w[ia], conv_pw2_b[ia])
            out_l = conformer_conv_module(xn_l, True, *pa)
            out_c = None if last else conformer_conv_module(xn_c, False, *pa)
            ia += 1
        else:
            out_l, out_c = rglru_block(
                xn_c, xn_l, not last,
                lru_wy[ib], lru_by[ib], lru_wx[ib], lru_bx[ib], lru_conv_w[ib], lru_conv_b[ib],
                lru_wa[ib], lru_ba[ib], lru_wi[ib], lru_bi[ib], lru_lambda[ib], lru_wo[ib], lru_bo[ib])
            ib += 1
        h_lat = h_lat + g1_l * out_l

        fn_l = modulate(rmsnorm(h_lat, norm2_g[layer]), sh2_l, sc2_l)
        moe_p = (router_w[layer], router_b[layer], moe_w1[layer], moe_b1[layer],
                 moe_w2[layer], moe_b2[layer])
        if last:
            h_lat = h_lat + g2_l * moe_ffn(fn_l, *moe_p)
        else:
            h_ctx = h_ctx + g1_c * out_c
            fn_c = modulate(rmsnorm(h_ctx, norm2_g[layer]), sh2_c, sc2_c)
            ff = moe_ffn(jnp.concatenate([fn_c, fn_l], axis=1), *moe_p)
            h_ctx = h_ctx + g2_c * ff[:, :CTX_LEN]
            h_lat = h_lat + g2_l * ff[:, CTX_LEN:]
    return rmsnorm(h_lat, final_g)
```

```python
import functools

import jax
import jax.numpy as jnp
from jax import lax
from jax.experimental import pallas as pl
from jax.experimental.pallas import tpu as pltpu

F32 = jnp.float32
BF16 = jnp.bfloat16

GRID_W = 64
N_ADA = 6
EPS = 1e-6
CONV_K = 31
CONV_PAD = CONV_K // 2
LRU_HEADS = 16
LRU_C = 8.0
LRU_CONV_K = 4
TOP_K = 4
SWIGLU_ALPHA = 1.702
SWIGLU_LIMIT = 7.0

ROW_TILE = 256
SUBLANES = 8
LANES = 128
VMEM_LIMIT = 56 * 1024 * 1024


def _cparams(*sem):
    return pltpu.CompilerParams(dimension_semantics=sem, vmem_limit_bytes=VMEM_LIMIT)


def _sigmoid(x):
    return 1.0 / (1.0 + jnp.exp(-x))


def _split_bf16(a):
    hi = a.astype(BF16)
    lo = (a - hi.astype(F32)).astype(BF16)
    return hi, lo


def _dot(a, b):
    return jnp.dot(a, b, preferred_element_type=F32)


def _dot_nt(a, b):
    return lax.dot_general(a, b, (((1,), (1,)), ((), ())), preferred_element_type=F32)


def _ada_kernel(s_ref, w_ref, b_ref, o_ref):
    s = s_ref[...]
    s = s * _sigmoid(s)
    s_hi, s_lo = _split_bf16(s)
    w_hi, w_lo = _split_bf16(w_ref[0])
    acc = _dot(s_hi, w_hi) + _dot(s_hi, w_lo) + _dot(s_lo, w_hi)
    o_ref[0] = acc + b_ref[0]


def _ada(src, ada_w, ada_b, tn=1024):
    depth, d, n = ada_w.shape
    return pl.pallas_call(
        _ada_kernel,
        grid=(depth, n // tn),
        in_specs=[
            pl.BlockSpec((SUBLANES, d), lambda l, j: (0, 0)),
            pl.BlockSpec((1, d, tn), lambda l, j: (l, 0, j)),
            pl.BlockSpec((1, 1, tn), lambda l, j: (l, 0, j)),
        ],
        out_specs=pl.BlockSpec((1, SUBLANES, tn), lambda l, j: (l, 0, j)),
        out_shape=jax.ShapeDtypeStruct((depth, SUBLANES, n), F32),
        compiler_params=_cparams("parallel", "parallel"),
    )(src, ada_w, ada_b.reshape(depth, 1, n))


def _rms_mod(x, g, sh, sc):
    ms = jnp.mean(x * x, axis=-1, keepdims=True)
    y = x * lax.rsqrt(ms + EPS) * g
    return y * (1.0 + sc) + sh


def _norm_mod_kernel(h_ref, g_ref, sh_ref, sc_ref, o_ref):
    o_ref[...] = _rms_mod(h_ref[...], g_ref[...], sh_ref[...], sc_ref[...]).astype(o_ref.dtype)


def _mod_spec(d, mod_index):
    return pl.BlockSpec((None, 1, d), lambda i, *_: (mod_index(i), 0, 0))


def _norm_mod(h, g, mods, sh_index, sc_index, n_tiles, tile_off=0):
    t, d = h.shape
    return pl.pallas_call(
        _norm_mod_kernel,
        grid=(n_tiles,),
        in_specs=[
            pl.BlockSpec((ROW_TILE, d), lambda i: (i + tile_off, 0)),
            pl.BlockSpec((1, d), lambda i: (0, 0)),
            _mod_spec(d, sh_index),
            _mod_spec(d, sc_index),
        ],
        out_specs=pl.BlockSpec((ROW_TILE, d), lambda i: (i, 0)),
        out_shape=jax.ShapeDtypeStruct((n_tiles * ROW_TILE, d), BF16),
        compiler_params=_cparams("parallel"),
    )(h, g.reshape(1, d), mods, mods)


def _layernorm_silu(x, g, b):
    mu = jnp.mean(x, axis=-1, keepdims=True)
    xc = x - mu
    y = xc * lax.rsqrt(jnp.mean(xc * xc, axis=-1, keepdims=True) + EPS)
    y = y * g + b
    return y * _sigmoid(y)


def _gelu_tanh(x):
    return 0.5 * x * (1.0 + jnp.tanh(0.7978845608028654 * (x + 0.044715 * (x * x * x))))


def _mm_plain_kernel(x_ref, w_ref, b_ref, o_ref, *, act):
    acc = _dot(x_ref[...], w_ref[...]) + b_ref[...]
    if act == "gelu":
        acc = _gelu_tanh(acc)
    o_ref[...] = acc


def _mm_glu_kernel(x_ref, wa_ref, wb_ref, ba_ref, bb_ref, o_ref):
    x = x_ref[...]
    ua = _dot(x, wa_ref[...]) + ba_ref[...]
    ub = _dot(x, wb_ref[...]) + bb_ref[...]
    o_ref[...] = ua * _sigmoid(ub)


def _mm_ln_res_kernel(x_ref, lg_ref, lb_ref, w_ref, b_ref, h_ref, gate_ref, o_ref, xs_ref):
    @pl.when(pl.program_id(1) == 0)
    def _():
        xs_ref[...] = _layernorm_silu(x_ref[...], lg_ref[...], lb_ref[...]).astype(BF16)

    acc = _dot(xs_ref[...], w_ref[...]) + b_ref[...]
    o_ref[...] = h_ref[...] + gate_ref[...] * acc


def _mm_res_kernel(x_ref, w_ref, b_ref, h_ref, gate_ref, o_ref):
    acc = _dot(x_ref[...], w_ref[...]) + b_ref[...]
    o_ref[...] = h_ref[...] + gate_ref[...] * acc


def _mm_plain(x, w, b, act=None, tm=512, tn=512, x_tile_off=0, n_rows=None):
    k, n = w.shape
    n_rows = x.shape[0] if n_rows is None else n_rows
    return pl.pallas_call(
        functools.partial(_mm_plain_kernel, act=act),
        grid=(n_rows // tm, n // tn),
        in_specs=[
            pl.BlockSpec((tm, k), lambda i, j: (i + x_tile_off, 0)),
            pl.BlockSpec((k, tn), lambda i, j: (0, j)),
            pl.BlockSpec((1, tn), lambda i, j: (0, j)),
        ],
        out_specs=pl.BlockSpec((tm, tn), lambda i, j: (i, j)),
        out_shape=jax.ShapeDtypeStruct((n_rows, n), F32),
        compiler_params=_cparams("parallel", "arbitrary"),
    )(x, w, b.reshape(1, n))


def _mm_glu(x, w, b, tm=512, tn=512):
    t, k = x.shape
    n = w.shape[1] // 2
    nj = n // tn
    b2 = b.reshape(1, 2 * n)
    return pl.pallas_call(
        _mm_glu_kernel,
        grid=(t // tm, nj),
        in_specs=[
            pl.BlockSpec((tm, k), lambda i, j: (i, 0)),
            pl.BlockSpec((k, tn), lambda i, j: (0, j)),
            pl.BlockSpec((k, tn), lambda i, j: (0, j + nj)),
            pl.BlockSpec((1, tn), lambda i, j: (0, j)),
            pl.BlockSpec((1, tn), lambda i, j: (0, j + nj)),
        ],
        out_specs=pl.BlockSpec((tm, tn), lambda i, j: (i, j)),
        out_shape=jax.ShapeDtypeStruct((t, n), F32),
        compiler_params=_cparams("parallel", "arbitrary"),
    )(x, w, w, b2, b2)


def _mm_ln_res(x, ln_g, ln_b, w, b, h, mods, gate_index, tm=512, tn=512):
    t, k = x.shape
    n = w.shape[1]
    return pl.pallas_call(
        _mm_ln_res_kernel,
        grid=(t // tm, n // tn),
        in_specs=[
            pl.BlockSpec((tm, k), lambda i, j: (i, 0)),
            pl.BlockSpec((1, k), lambda i, j: (0, 0)),
            pl.BlockSpec((1, k), lambda i, j: (0, 0)),
            pl.BlockSpec((k, tn), lambda i, j: (0, j)),
            pl.BlockSpec((1, tn), lambda i, j: (0, j)),
            pl.BlockSpec((tm, tn), lambda i, j: (i, j)),
            pl.BlockSpec((None, 1, tn), lambda i, j: (gate_index(i), 0, j)),
        ],
        out_specs=pl.BlockSpec((tm, tn), lambda i, j: (i, j)),
        out_shape=jax.ShapeDtypeStruct((t, n), F32),
        scratch_shapes=[pltpu.VMEM((tm, k), BF16)],
        compiler_params=_cparams("parallel", "arbitrary"),
    )(x, ln_g.reshape(1, k), ln_b.reshape(1, k), w, b.reshape(1, n), h, mods)


def _mm_res(x, w, b, h, mods, gate_index, h_tile_off, tm=512, tn=512):
    t, k = x.shape
    n = w.shape[1]
    return pl.pallas_call(
        _mm_res_kernel,
        grid=(t // tm, n // tn),
        in_specs=[
            pl.BlockSpec((tm, k), lambda i, j: (i, 0)),
            pl.BlockSpec((k, tn), lambda i, j: (0, j)),
            pl.BlockSpec((1, tn), lambda i, j: (0, j)),
            pl.BlockSpec((tm, tn), lambda i, j: (i + h_tile_off, j)),
            pl.BlockSpec((None, 1, tn), lambda i, j: (gate_index(i), 0, j)),
        ],
        out_specs=pl.BlockSpec((tm, tn), lambda i, j: (i, j)),
        out_shape=jax.ShapeDtypeStruct((t, n), F32),
        compiler_params=_cparams("parallel", "arbitrary"),
    )(x, w, b.reshape(1, n), h, mods)


SEG_PAD = 16


def _conv_seg_kernel(v_ref, w_ref, b_ref, prev_ref, o_ref, pad_ref, *, seg, n_seg):
    del prev_ref
    tc = v_ref.shape[1]
    zeros = jnp.zeros((SEG_PAD, tc), F32)
    for s in range(n_seg):
        pad_ref[s, pl.ds(0, SEG_PAD), :] = zeros
        pad_ref[s, pl.ds(SEG_PAD, seg), :] = v_ref[pl.ds(s * seg, seg), :]
        pad_ref[s, pl.ds(SEG_PAD + seg, SEG_PAD), :] = zeros
    w = w_ref[...]
    bias = b_ref[...]
    chunk = min(seg, 64)
    for s in range(n_seg):
        for c0 in range(0, seg, chunk):
            acc = jnp.zeros((chunk, tc), F32) + bias
            for k in range(CONV_K):
                off = SEG_PAD - CONV_PAD + k + c0
                acc = acc + w[k:k + 1, :] * pad_ref[s, pl.ds(off, chunk), :]
            o_ref[pl.ds(s * seg + c0, chunk), :] = acc


def _conv_seg(v, w, b, prev, *, seg, row0, n_rows, ch0, n_ch, rows_blk=512, tc=128):
    t, d = v.shape
    n_seg = rows_blk // seg
    r_off, c_off = row0 // rows_blk, ch0 // tc
    blk = lambda i, j: (i + r_off, j + c_off)
    return pl.pallas_call(
        functools.partial(_conv_seg_kernel, seg=seg, n_seg=n_seg),
        grid=(n_rows // rows_blk, n_ch // tc),
        in_specs=[
            pl.BlockSpec((rows_blk, tc), blk),
            pl.BlockSpec((CONV_K, tc), lambda i, j: (0, j + c_off)),
            pl.BlockSpec((1, tc), lambda i, j: (0, j + c_off)),
            pl.BlockSpec(memory_space=pl.ANY),
        ],
        out_specs=pl.BlockSpec((rows_blk, tc), blk),
        out_shape=jax.ShapeDtypeStruct((t, d), F32),
        scratch_shapes=[pltpu.VMEM((n_seg, seg + 2 * SEG_PAD, tc), F32)],
        input_output_aliases={3: 0},
        compiler_params=_cparams("parallel", "parallel"),
    )(v, w, b.reshape(1, d), prev)


def _conv_col_kernel(v_ref, w_ref, b_ref, prev_ref, o_ref, pad_ref, *, stride):
    del prev_ref
    n, tc = v_ref.shape
    halo = CONV_PAD * stride
    pad_ref[pl.ds(0, halo), :] = jnp.zeros((halo, tc), F32)
    pad_ref[pl.ds(halo, n), :] = v_ref[...]
    pad_ref[pl.ds(halo + n, halo), :] = jnp.zeros((halo, tc), F32)
    w = w_ref[...]
    bias = b_ref[...]

    def body(c, carry):
        base = pl.multiple_of(c * stride, stride)
        acc = jnp.zeros((stride, tc), F32) + bias
        for k in range(CONV_K):
            acc = acc + w[k:k + 1, :] * pad_ref[pl.ds(base + k * stride, stride), :]
        o_ref[pl.ds(base, stride), :] = acc
        return carry

    lax.fori_loop(0, n // stride, body, 0)


def _conv_col(v, w, b, prev, *, stride, row0, n_col_rows, n_cols, ch0, n_ch, tc=128):
    t, d = v.shape
    r_off, c_off = row0 // n_col_rows, ch0 // tc
    blk = lambda i, j: (i + r_off, j + c_off)
    return pl.pallas_call(
        functools.partial(_conv_col_kernel, stride=stride),
        grid=(n_cols, n_ch // tc),
        in_specs=[
            pl.BlockSpec((n_col_rows, tc), blk),
            pl.BlockSpec((CONV_K, tc), lambda i, j: (0, j + c_off)),
            pl.BlockSpec((1, tc), lambda i, j: (0, j + c_off)),
            pl.BlockSpec(memory_space=pl.ANY),
        ],
        out_specs=pl.BlockSpec((n_col_rows, tc), blk),
        out_shape=jax.ShapeDtypeStruct((t, d), F32),
        scratch_shapes=[pltpu.VMEM((n_col_rows + 2 * CONV_PAD * stride, tc), F32)],
        input_output_aliases={3: 0},
        compiler_params=_cparams("parallel", "parallel"),
    )(v, w, b.reshape(1, d), prev)


def _scan_rows(a, u, carry, reverse):
    n = a.shape[0]
    row = lax.broadcasted_iota(jnp.int32, (SUBLANES, a.shape[1]), 0)
    outs = [None] * (n // SUBLANES)
    order = range(n // SUBLANES - 1, -1, -1) if reverse else range(n // SUBLANES)
    for g in order:
        ag = a[g * SUBLANES:(g + 1) * SUBLANES]
        ug = u[g * SUBLANES:(g + 1) * SUBLANES]
        for s in (1, 2, 4):
            if reverse:
                a_sh = pltpu.roll(ag, SUBLANES - s, 0)
                u_sh = pltpu.roll(ug, SUBLANES - s, 0)
                ok = row < SUBLANES - s
            else:
                a_sh = pltpu.roll(ag, s, 0)
                u_sh = pltpu.roll(ug, s, 0)
                ok = row >= s
            ug = jnp.where(ok, ag * u_sh + ug, ug)
            ag = jnp.where(ok, ag * a_sh, ag)
        hg = ug + ag * carry
        carry = hg[0:1] if reverse else hg[SUBLANES - 1:SUBLANES]
        outs[g] = hg
    return jnp.concatenate(outs, axis=0), carry


def _lru_kernel(p_ref, pprev_ref, pnext_ref, cw_ref, cb_ref, wa_ref, ba_ref, wi_ref, bi_ref,
                lam_ref, *rest, reverse, n_steps, final):
    if final:
        hin_ref, y_ref, o_ref, pad_ref, carry_ref = rest
    else:
        o_ref, pad_ref, carry_ref = rest
    step = pl.program_id(1)
    ts, d = p_ref.shape
    blk = d // LRU_HEADS

    @pl.when(step == 0)
    def _():
        carry_ref[...] = jnp.zeros_like(carry_ref)

    if reverse:
        has_prev = jnp.logical_and(step >= 1, step < n_steps - 1)
        has_next = step >= 2
    else:
        has_prev = step >= 2
        has_next = jnp.logical_and(step >= 1, step < n_steps - 1)
    pad_ref[pl.ds(0, SUBLANES), :] = jnp.where(has_prev, pprev_ref[...], 0.0)
    pad_ref[pl.ds(SUBLANES, ts), :] = p_ref[...]
    pad_ref[pl.ds(SUBLANES + ts, SUBLANES), :] = jnp.where(has_next, pnext_ref[...], 0.0)

    for hd in range(LRU_HEADS):
        cs = slice(hd * blk, (hd + 1) * blk)
        xb = jnp.zeros((ts, blk), F32) + cb_ref[:, cs]
        for k in range(LRU_CONV_K):
            xb = xb + cw_ref[k:k + 1, cs] * pad_ref[pl.ds(SUBLANES - 1 + k, ts), cs]
        xh = xb.astype(BF16)
        r = _sigmoid(_dot(xh, wa_ref[hd]) + ba_ref[:, cs])
        ig = _sigmoid(_dot(xh, wi_ref[hd]) + bi_ref[:, cs])
        z = -lam_ref[:, cs]
        softplus = jnp.maximum(z, 0.0) + jnp.log(1.0 + jnp.exp(-jnp.abs(z)))
        log_a = -LRU_C * r * softplus
        a = jnp.exp(log_a)
        mult = jnp.sqrt(1.0 - jnp.exp(2.0 * log_a))
        u = xb * ig * mult
        hs, carry = _scan_rows(a, u, carry_ref[:, cs], reverse)
        carry_ref[:, cs] = carry
        if final:
            o_ref[:, cs] = (y_ref[:, cs] * (hin_ref[:, cs] + hs)).astype(o_ref.dtype)
        else:
            o_ref[:, cs] = hs


def _lru_direction(p, cw, cb, wa, ba, wi, bi, lam, *, reverse, n_batch, ctx_len, seq, h_in=None, y=None):
    t, d = p.shape
    ts = ROW_TILE
    assert ctx_len == ts
    lat_tiles = seq // ts
    n_steps = lat_tiles + 1
    ctx_tiles = n_batch
    sub_per_tile = ts // SUBLANES
    n_sub = t // SUBLANES
    final = h_in is not None

    def lat_idx(s):
        return (lat_tiles - s) if reverse else (s - 1)

    def tile(b, s):
        return jnp.where(s == 0, n_batch * lat_tiles + b, out_tile(b, s))

    def out_tile(b, s):
        return b * lat_tiles + lat_idx(jnp.maximum(s, 1))

    p_spec = pl.BlockSpec((ts, d), lambda b, s: (tile(b, s), 0))
    prev_spec = pl.BlockSpec((SUBLANES, d), lambda b, s: (jnp.maximum(tile(b, s) * sub_per_tile - 1, 0), 0))
    next_spec = pl.BlockSpec((SUBLANES, d), lambda b, s: (jnp.minimum((tile(b, s) + 1) * sub_per_tile, n_sub - 1), 0))
    vec = pl.BlockSpec((1, d), lambda b, s: (0, 0))
    blk = d // LRU_HEADS
    wspec = pl.BlockSpec((LRU_HEADS, blk, blk), lambda b, s: (0, 0, 0))
    out_spec = pl.BlockSpec((ts, d), lambda b, s: (out_tile(b, s), 0))
    in_specs = [p_spec, prev_spec, next_spec,
                pl.BlockSpec((LRU_CONV_K, d), lambda b, s: (0, 0)), vec,
                wspec, vec, wspec, vec, vec]
    args = [p, p, p, cw, cb.reshape(1, d), wa, ba.reshape(1, d), wi, bi.reshape(1, d), lam.reshape(1, d)]
    if final:
        in_specs += [out_spec, out_spec]
        args += [h_in, y]
    return pl.pallas_call(
        functools.partial(_lru_kernel, reverse=reverse, n_steps=n_steps, final=final),
        grid=(n_batch, n_steps),
        in_specs=in_specs,
        out_specs=out_spec,
        out_shape=jax.ShapeDtypeStruct((n_batch * seq, d), BF16 if final else F32),
        scratch_shapes=[pltpu.VMEM((ts + 2 * SUBLANES, d), F32), pltpu.VMEM((1, d), F32)],
        compiler_params=_cparams("arbitrary", "arbitrary"),
    )(*args)


def _pack_bf16_pairs(y):
    n = y.shape[1] // 2
    hi = pltpu.bitcast(y[:, :n].astype(BF16).astype(F32), jnp.uint32)
    lo = pltpu.bitcast(y[:, n:].astype(BF16).astype(F32), jnp.uint32)
    return hi | (lo >> 16)


def _unpack_bf16_pairs(p):
    hi = pltpu.bitcast(p & jnp.uint32(0xFFFF0000), F32).astype(BF16)
    lo = pltpu.bitcast(p << 16, F32).astype(BF16)
    return hi, lo


def _router_kernel(h_ref, g_ref, sh_ref, sc_ref, rw_ref, rb_ref,
                   xp_ref, idx_ref, prob_ref, rank_ref, cnt_ref, carry_ref):
    i = pl.program_id(0)
    tm = h_ref.shape[0]
    n_e = rw_ref.shape[0]

    @pl.when(i == 0)
    def _():
        carry_ref[...] = jnp.zeros_like(carry_ref)

    fn = _rms_mod(h_ref[...], g_ref[...], sh_ref[...], sc_ref[...])
    xp_ref[...] = _pack_bf16_pairs(fn)

    f_hi, f_lo = _split_bf16(fn)
    w_hi, w_lo = _split_bf16(rw_ref[...])
    logits = _dot_nt(w_hi, f_hi) + _dot_nt(w_lo, f_hi) + _dot_nt(w_hi, f_lo) + rb_ref[...]

    e_iota = lax.broadcasted_iota(jnp.int32, (n_e, tm), 0).astype(F32)
    taken = jnp.zeros((n_e, tm), F32)
    neg_inf = jnp.float32(-jnp.inf)
    sels, vals, idxs = [], [], []
    for _ in range(TOP_K):
        masked = jnp.where(taken > 0.0, neg_inf, logits)
        m = jnp.max(masked, axis=0, keepdims=True)
        cand = jnp.logical_and(masked == m, taken == 0.0)
        idx = jnp.min(jnp.where(cand, e_iota, float(n_e)), axis=0, keepdims=True)
        sel = e_iota == idx
        taken = jnp.where(sel, 1.0, taken)
        sels.append(sel)
        vals.append(m)
        idxs.append(idx)
    exps = [jnp.exp(v - vals[0]) for v in vals]
    denom = exps[0] + exps[1] + exps[2] + exps[3]
    onehot = jnp.zeros((n_e, tm), F32)
    for sel in sels:
        onehot = onehot + jnp.where(sel, 1.0, 0.0)

    r_iota = lax.broadcasted_iota(jnp.int32, (tm, tm), 0)
    c_iota = lax.broadcasted_iota(jnp.int32, (tm, tm), 1)
    upper = jnp.where(r_iota < c_iota, 1.0, 0.0).astype(BF16)
    before = _dot(onehot.astype(BF16), upper) + carry_ref[:, 0:1]
    for k in range(TOP_K):
        idx_ref[pl.ds(k, 1), :] = idxs[k].astype(jnp.int32)
        prob_ref[pl.ds(k, 1), :] = exps[k] / denom
        rank_ref[pl.ds(k, 1), :] = jnp.sum(jnp.where(sels[k], before, 0.0), axis=0,
                                           keepdims=True).astype(jnp.int32)
    carry_ref[...] = carry_ref[...] + jnp.sum(onehot, axis=1, keepdims=True)
    cnt_ref[...] = carry_ref[...]


def _router(h, g, mods, sh_index, sc_index, rw_t, rb, n_tiles, tile_off):
    t, d = h.shape
    n_e = rw_t.shape[0]
    tm = ROW_TILE
    n_tok = n_tiles * tm
    small = lambda dt: jax.ShapeDtypeStruct((TOP_K, n_tok), dt)
    small_spec = pl.BlockSpec((TOP_K, tm), lambda i: (0, i))
    return pl.pallas_call(
        _router_kernel,
        grid=(n_tiles,),
        in_specs=[
            pl.BlockSpec((tm, d), lambda i: (i + tile_off, 0)),
            pl.BlockSpec((1, d), lambda i: (0, 0)),
            _mod_spec(d, sh_index),
            _mod_spec(d, sc_index),
            pl.BlockSpec((n_e, d), lambda i: (0, 0)),
            pl.BlockSpec((n_e, 1), lambda i: (0, 0)),
        ],
        out_specs=[
            pl.BlockSpec((tm, d // 2), lambda i: (i, 0)),
            small_spec, small_spec, small_spec,
            pl.BlockSpec((n_e, LANES), lambda i: (0, 0)),
        ],
        out_shape=[
            jax.ShapeDtypeStruct((n_tok, d // 2), jnp.uint32),
            small(jnp.int32), small(F32), small(jnp.int32),
            jax.ShapeDtypeStruct((n_e, LANES), F32),
        ],
        scratch_shapes=[pltpu.VMEM((n_e, LANES), F32)],
        compiler_params=_cparams("arbitrary"),
    )(h, g.reshape(1, d), mods, mods, rw_t, rb.reshape(n_e, 1))


def _dispatch_kernel(pos_ref, x_ref, zeros_ref, o_ref, sem):
    del zeros_ref
    tm = x_ref.shape[0]

    def copy(r, k):
        return pltpu.make_async_copy(x_ref.at[pl.ds(r, 1)], o_ref.at[pl.ds(pos_ref[0, k, r], 1)], sem)

    def start(r, c):
        for k in range(TOP_K):
            copy(r, k).start()
        return c

    def wait(r, c):
        for k in range(TOP_K):
            copy(r, k).wait()
        return c

    lax.fori_loop(0, tm, start, 0)
    lax.fori_loop(0, tm, wait, 0)


def _dispatch(xp, pos_tiles, n_sorted):
    t, w = xp.shape
    tm = ROW_TILE
    zeros = jnp.zeros((n_sorted, w), jnp.uint32)
    return pl.pallas_call(
        _dispatch_kernel,
        grid=(t // tm,),
        in_specs=[
            pl.BlockSpec((1, TOP_K, tm), lambda i: (i, 0, 0), memory_space=pltpu.SMEM),
            pl.BlockSpec((tm, w), lambda i: (i, 0)),
            pl.BlockSpec(memory_space=pl.ANY),
        ],
        out_specs=pl.BlockSpec(memory_space=pl.ANY),
        out_shape=jax.ShapeDtypeStruct((n_sorted, w), jnp.uint32),
        scratch_shapes=[pltpu.SemaphoreType.DMA(())],
        input_output_aliases={2: 0},
        compiler_params=_cparams("arbitrary"),
    )(pos_tiles, xp, zeros)


def _combine_kernel(pos_ref, y_ref, p_ref, h_ref, gate_ref, fg_ref, o_ref, buf_ref, sem, *, final_norm):
    tm = h_ref.shape[0]

    def copy(r, k):
        return pltpu.make_async_copy(y_ref.at[pl.ds(pos_ref[0, k, r], 1)], buf_ref.at[k, pl.ds(r, 1)], sem)

    def start(r, c):
        for k in range(TOP_K):
            copy(r, k).start()
        return c

    def wait(r, c):
        for k in range(TOP_K):
            copy(r, k).wait()
        return c

    lax.fori_loop(0, tm, start, 0)
    lax.fori_loop(0, tm, wait, 0)
    p = p_ref[...]
    acc = p[:, 0:1] * buf_ref[0]
    for k in range(1, TOP_K):
        acc = acc + p[:, k:k + 1] * buf_ref[k]
    out = h_ref[...] + gate_ref[...] * acc
    if final_norm:
        ms = jnp.mean(out * out, axis=-1, keepdims=True)
        out = out * lax.rsqrt(ms + EPS) * fg_ref[...]
    o_ref[...] = out


def _combine(y, pos_tiles, probs, h, mods, gate_index, final_g, h_tile_off, final_norm, tm=128):
    n_tok = probs.shape[0]
    d = y.shape[1]
    return pl.pallas_call(
        functools.partial(_combine_kernel, final_norm=final_norm),
        grid=(n_tok // tm,),
        in_specs=[
            pl.BlockSpec((1, TOP_K, tm), lambda i: (i, 0, 0), memory_space=pltpu.SMEM),
            pl.BlockSpec(memory_space=pl.ANY),
            pl.BlockSpec((tm, TOP_K), lambda i: (i, 0)),
            pl.BlockSpec((tm, d), lambda i: (i + h_tile_off, 0)),
            _mod_spec(d, gate_index),
            pl.BlockSpec((1, d), lambda i: (0, 0)),
        ],
        out_specs=pl.BlockSpec((tm, d), lambda i: (i, 0)),
        out_shape=jax.ShapeDtypeStruct((n_tok, d), F32),
        scratch_shapes=[pltpu.VMEM((TOP_K, tm, d), F32), pltpu.SemaphoreType.DMA(())],
        compiler_params=_cparams("arbitrary"),
    )(pos_tiles, y, probs, h, mods, final_g.reshape(1, d))


def _expert_up_kernel(te_ref, tv_ref, x_ref, wa_ref, wb_ref, ba_ref, bb_ref, o_ref, wbf_ref):
    i = pl.program_id(1)
    half = x_ref.shape[1]
    fresh = jnp.logical_or(i == 0, te_ref[i] != te_ref[jnp.maximum(i - 1, 0)])

    @pl.when(fresh)
    def _():
        wbf_ref[0] = wa_ref[0].astype(BF16)
        wbf_ref[1] = wb_ref[0].astype(BF16)

    @pl.when(tv_ref[i] > 0)
    def _():
        x_hi, x_lo = _unpack_bf16_pairs(x_ref[...])
        glu = _dot(x_hi, wbf_ref[0, :half, :]) + _dot(x_lo, wbf_ref[0, half:, :]) + ba_ref[0]
        lin = _dot(x_hi, wbf_ref[1, :half, :]) + _dot(x_lo, wbf_ref[1, half:, :]) + bb_ref[0]
        glu = jnp.minimum(glu, SWIGLU_LIMIT)
        lin = jnp.clip(lin, -SWIGLU_LIMIT, SWIGLU_LIMIT)
        o_ref[...] = (glu * _sigmoid(SWIGLU_ALPHA * glu) * (lin + 1.0)).astype(o_ref.dtype)

    @pl.when(tv_ref[i] == 0)
    def _():
        o_ref[...] = jnp.zeros_like(o_ref)


def _expert_up(xs, tile_expert, tile_valid, w1, b1, tn=512):
    n_sorted, half = xs.shape
    n_e, d, two_f = w1.shape
    f = two_f // 2
    nj = f // tn
    tm = ROW_TILE
    n_tiles = n_sorted // tm
    b1r = b1.reshape(n_e, 1, two_f)
    grid_spec = pltpu.PrefetchScalarGridSpec(
        num_scalar_prefetch=2,
        grid=(nj, n_tiles),
        in_specs=[
            pl.BlockSpec((tm, half), lambda j, i, te, tv: (i, 0)),
            pl.BlockSpec((1, d, tn), lambda j, i, te, tv: (te[i], 0, j)),
            pl.BlockSpec((1, d, tn), lambda j, i, te, tv: (te[i], 0, j + nj)),
            pl.BlockSpec((1, 1, tn), lambda j, i, te, tv: (te[i], 0, j)),
            pl.BlockSpec((1, 1, tn), lambda j, i, te, tv: (te[i], 0, j + nj)),
        ],
        out_specs=pl.BlockSpec((tm, tn), lambda j, i, te, tv: (i, j)),
        scratch_shapes=[pltpu.VMEM((2, d, tn), BF16)],
    )
    return pl.pallas_call(
        _expert_up_kernel,
        grid_spec=grid_spec,
        out_shape=jax.ShapeDtypeStruct((n_sorted, f), BF16),
        compiler_params=_cparams("arbitrary", "arbitrary"),
    )(tile_expert, tile_valid, xs, w1, w1, b1r, b1r)


def _expert_down_kernel(te_ref, tv_ref, a_ref, w_ref, b_ref, o_ref, wbf_ref):
    i = pl.program_id(1)
    fresh = jnp.logical_or(i == 0, te_ref[i] != te_ref[jnp.maximum(i - 1, 0)])

    @pl.when(fresh)
    def _():
        wbf_ref[...] = w_ref[0].astype(BF16)

    @pl.when(tv_ref[i] > 0)
    def _():
        o_ref[...] = _dot(a_ref[...], wbf_ref[...]) + b_ref[0]

    @pl.when(tv_ref[i] == 0)
    def _():
        o_ref[...] = jnp.zeros_like(o_ref)


def _expert_down(act, tile_expert, tile_valid, w2, b2, tn=1024):
    n_sorted, f = act.shape
    n_e, _, d = w2.shape
    nj = d // tn
    tm = ROW_TILE
    n_tiles = n_sorted // tm
    grid_spec = pltpu.PrefetchScalarGridSpec(
        num_scalar_prefetch=2,
        grid=(nj, n_tiles),
        in_specs=[
            pl.BlockSpec((tm, f), lambda j, i, te, tv: (i, 0)),
            pl.BlockSpec((1, f, tn), lambda j, i, te, tv: (te[i], 0, j)),
            pl.BlockSpec((1, 1, tn), lambda j, i, te, tv: (te[i], 0, j)),
        ],
        out_specs=pl.BlockSpec((tm, tn), lambda j, i, te, tv: (i, j)),
        scratch_shapes=[pltpu.VMEM((f, tn), BF16)],
    )
    return pl.pallas_call(
        _expert_down_kernel,
        grid_spec=grid_spec,
        out_shape=jax.ShapeDtypeStruct((n_sorted, d), F32),
        compiler_params=_cparams("arbitrary", "arbitrary"),
    )(tile_expert, tile_valid, act, w2, b2.reshape(n_e, 1, d))


def _moe(h, g, mods, mod_index, router_w, router_b, w1, b1, w2, b2, final_g, n_tiles, final_norm):
    tm = ROW_TILE
    n_e = router_w.shape[1]
    n_tok = n_tiles * tm
    xp, idx, prob, rank, cnt = _router(h, g, mods, mod_index(3, tm), mod_index(4, tm), router_w.T, router_b,
                                       n_tiles, 0)

    counts = cnt[:, 0].astype(jnp.int32)
    padded = ((counts + tm - 1) // tm) * tm
    ends = jnp.cumsum(padded)
    starts = ends - padded
    pos = starts[idx] + rank
    n_tiles_sorted = (n_tok * TOP_K) // tm + n_e
    tile_start = jnp.arange(n_tiles_sorted, dtype=jnp.int32) * tm
    tile_valid = (tile_start < ends[-1]).astype(jnp.int32)
    last_valid = jnp.maximum(ends[-1] // tm - 1, 0) * tm
    tile_expert = jnp.searchsorted(ends, jnp.minimum(tile_start, last_valid), side="right").astype(jnp.int32)
    tile_expert = jnp.minimum(tile_expert, n_e - 1)

    def per_tile(a, rows):
        return a.reshape(TOP_K, n_tok // rows, rows).transpose(1, 0, 2)

    xs = _dispatch(xp, per_tile(pos, tm), n_tiles_sorted * tm)
    act = _expert_up(xs, tile_expert, tile_valid, w1, b1)
    y = _expert_down(act, tile_expert, tile_valid, w2, b2)
    tmc = 128
    return _combine(y, per_tile(pos, tmc), prob.T, h, mods, mod_index(5, tmc), final_g, 0, final_norm, tm=tmc)


def kernel(x, c, ctx, c_ctx, ada_w, ada_b, norm1_g, norm2_g, conv_pw1_w, conv_pw1_b, conv_dw_w, conv_dw_b, conv_ln_g, conv_ln_b, conv_pw2_w, conv_pw2_b, lru_wy, lru_by, lru_wx, lru_bx, lru_conv_w, lru_conv_b, lru_wa, lru_ba, lru_wi, lru_bi, lru_lambda, lru_wo, lru_bo, router_w, router_b, moe_w1, moe_b1, moe_w2, moe_b2, final_g):
    n_batch, seq, d = x.shape
    ctx_len = ctx.shape[1]
    depth = ada_w.shape[0]
    tm = ROW_TILE
    n_ctx, n_lat = n_batch * ctx_len, n_batch * seq
    t_all = n_ctx + n_lat
    lat_tiles, all_tiles = n_lat // tm, t_all // tm
    ctx_src = n_batch
    n_src = n_batch + 1
    big = 512
    assert depth == 2 and ctx_len == tm and seq % big == 0 and n_ctx % big == 0 and n_src <= SUBLANES

    src = jnp.zeros((SUBLANES, d), F32).at[:n_batch].set(c).at[ctx_src].set(c_ctx)
    mods = _ada(src, ada_w, ada_b)[:, :n_src].reshape(depth * n_src * N_ADA, 1, d)

    def mod_index(layer):
        def for_chunk(chunk, rows):
            def index(i):
                row = i * rows
                tile_src = jnp.where(row >= n_lat, ctx_src, row // seq)
                return (layer * n_src + tile_src) * N_ADA + chunk
            return index
        return for_chunk

    bf = lambda a: a.astype(BF16)
    h = jnp.concatenate([x.reshape(n_lat, d), ctx.reshape(n_ctx, d)], axis=0)

    mi = mod_index(0)
    xn = _norm_mod(h, norm1_g[0], mods, mi(0, tm), mi(1, tm), all_tiles)
    v = _mm_glu(xn, bf(conv_pw1_w[0]), conv_pw1_b[0], tm=big)
    half = d // 2
    dw_w, dw_b = conv_dw_w[0], conv_dw_b[0]
    cv = jnp.zeros((t_all, d), F32)
    cv = _conv_seg(v, dw_w, dw_b, cv, seg=ctx_len, row0=n_lat, n_rows=n_ctx, ch0=0, n_ch=d, rows_blk=ctx_len)
    cv = _conv_seg(v, dw_w, dw_b, cv, seg=GRID_W, row0=0, n_rows=n_lat, ch0=0, n_ch=half)
    cv = _conv_col(v, dw_w, dw_b, cv, stride=GRID_W, row0=0, n_col_rows=seq, n_cols=n_batch, ch0=half, n_ch=half)
    h = _mm_ln_res(cv, conv_ln_g[0], conv_ln_b[0], bf(conv_pw2_w[0]), conv_pw2_b[0], h, mods, mi(2, big), tm=big)
    h = _moe(h, norm2_g[0], mods, mi, router_w[0], router_b[0], moe_w1[0], moe_b1[0], moe_w2[0], moe_b2[0],
             final_g, all_tiles, False)

    mi = mod_index(1)
    xn = _norm_mod(h, norm1_g[1], mods, mi(0, tm), mi(1, tm), all_tiles)
    p = _mm_plain(xn, bf(lru_wx[0]), lru_bx[0], tm=big)
    yl = _mm_plain(xn, bf(lru_wy[0]), lru_by[0], act="gelu", tm=big, n_rows=n_lat)
    common = dict(n_batch=n_batch, ctx_len=ctx_len, seq=seq)
    hf = _lru_direction(p, lru_conv_w[0], lru_conv_b[0], bf(lru_wa[0, 0]), lru_ba[0, 0], bf(lru_wi[0, 0]),
                        lru_bi[0, 0], lru_lambda[0, 0], reverse=False, **common)
    z = _lru_direction(p, lru_conv_w[0], lru_conv_b[0], bf(lru_wa[0, 1]), lru_ba[0, 1], bf(lru_wi[0, 1]),
                       lru_bi[0, 1], lru_lambda[0, 1], reverse=True, h_in=hf, y=yl, **common)
    h_lat = _mm_res(z, bf(lru_wo[0]), lru_bo[0], h, mods, mi(2, big), 0, tm=big)
    out = _moe(h_lat, norm2_g[1], mods, mi, router_w[1], router_b[1], moe_w1[1], moe_b1[1], moe_w2[1], moe_b2[1],
               final_g, lat_tiles, True)
    return out.reshape(n_batch, seq, d)
```

```python
import functools

import jax
import jax.numpy as jnp
from jax import lax
from jax.experimental import pallas as pl
from jax.experimental.pallas import tpu as pltpu

F32 = jnp.float32
BF16 = jnp.bfloat16

GRID_W = 64
N_ADA = 6
EPS = 1e-6
CONV_K = 31
CONV_PAD = CONV_K // 2
LRU_HEADS = 16
LRU_C = 8.0
LRU_CONV_K = 4
TOP_K = 4
SWIGLU_ALPHA = 1.702
SWIGLU_LIMIT = 7.0

ROW_TILE = 256
SUBLANES = 8
LANES = 128
VMEM_LIMIT = 56 * 1024 * 1024


def _cparams(*sem):
    return pltpu.CompilerParams(dimension_semantics=sem, vmem_limit_bytes=VMEM_LIMIT)


def _sigmoid(x):
    return 1.0 / (1.0 + jnp.exp(-x))


def _split_bf16(a):
    hi = a.astype(BF16)
    lo = (a - hi.astype(F32)).astype(BF16)
    return hi, lo


def _dot(a, b):
    return jnp.dot(a, b, preferred_element_type=F32)


def _dot_nt(a, b):
    return lax.dot_general(a, b, (((1,), (1,)), ((), ())), preferred_element_type=F32)


def _ada_kernel(s_ref, w_ref, b_ref, o_ref):
    s = s_ref[...]
    s = s * _sigmoid(s)
    s_hi, s_lo = _split_bf16(s)
    w_hi, w_lo = _split_bf16(w_ref[0])
    acc = _dot(s_hi, w_hi) + _dot(s_hi, w_lo) + _dot(s_lo, w_hi)
    o_ref[0] = acc + b_ref[0]


def _ada(src, ada_w, ada_b, tn=1024):
    depth, d, n = ada_w.shape
    return pl.pallas_call(
        _ada_kernel,
        grid=(depth, n // tn),
        in_specs=[
            pl.BlockSpec((SUBLANES, d), lambda l, j: (0, 0)),
            pl.BlockSpec((1, d, tn), lambda l, j: (l, 0, j)),
            pl.BlockSpec((1, 1, tn), lambda l, j: (l, 0, j)),
        ],
        out_specs=pl.BlockSpec((1, SUBLANES, tn), lambda l, j: (l, 0, j)),
        out_shape=jax.ShapeDtypeStruct((depth, SUBLANES, n), F32),
        compiler_params=_cparams("parallel", "parallel"),
    )(src, ada_w, ada_b.reshape(depth, 1, n))


def _rms_mod(x, g, sh, sc):
    ms = jnp.mean(x * x, axis=-1, keepdims=True)
    y = x * lax.rsqrt(ms + EPS) * g
    return y * (1.0 + sc) + sh


def _norm_mod_kernel(h_ref, g_ref, sh_ref, sc_ref, o_ref):
    o_ref[...] = _rms_mod(h_ref[...], g_ref[...], sh_ref[...], sc_ref[...]).astype(o_ref.dtype)


def _mod_spec(d, mod_index):
    return pl.BlockSpec((None, 1, d), lambda i, *_: (mod_index(i), 0, 0))


def _norm_mod(h, g, mods, sh_index, sc_index, n_tiles, tile_off=0):
    t, d = h.shape
    return pl.pallas_call(
        _norm_mod_kernel,
        grid=(n_tiles,),
        in_specs=[
            pl.BlockSpec((ROW_TILE, d), lambda i: (i + tile_off, 0)),
            pl.BlockSpec((1, d), lambda i: (0, 0)),
            _mod_spec(d, sh_index),
            _mod_spec(d, sc_index),
        ],
        out_specs=pl.BlockSpec((ROW_TILE, d), lambda i: (i, 0)),
        out_shape=jax.ShapeDtypeStruct((n_tiles * ROW_TILE, d), BF16),
        compiler_params=_cparams("parallel"),
    )(h, g.reshape(1, d), mods, mods)


def _layernorm_silu(x, g, b):
    mu = jnp.mean(x, axis=-1, keepdims=True)
    xc = x - mu
    y = xc * lax.rsqrt(jnp.mean(xc * xc, axis=-1, keepdims=True) + EPS)
    y = y * g + b
    return y * _sigmoid(y)


def _gelu_tanh(x):
    return 0.5 * x * (1.0 + jnp.tanh(0.7978845608028654 * (x + 0.044715 * (x * x * x))))


def _mm_pair_kernel(x_ref, wa_ref, wb_ref, ba_ref, bb_ref, oa_ref, ob_ref):
    x = x_ref[...]
    oa_ref[...] = _dot(x, wa_ref[...]) + ba_ref[...]
    ob_ref[...] = _gelu_tanh(_dot(x, wb_ref[...]) + bb_ref[...])


def _mm_glu_kernel(x_ref, wa_ref, wb_ref, ba_ref, bb_ref, o_ref):
    x = x_ref[...]
    ua = _dot(x, wa_ref[...]) + ba_ref[...]
    ub = _dot(x, wb_ref[...]) + bb_ref[...]
    o_ref[...] = ua * _sigmoid(ub)


def _mm_ln_res_kernel(x_ref, lg_ref, lb_ref, w_ref, b_ref, h_ref, gate_ref, o_ref, xs_ref):
    @pl.when(pl.program_id(1) == 0)
    def _():
        xs_ref[...] = _layernorm_silu(x_ref[...], lg_ref[...], lb_ref[...]).astype(BF16)

    acc = _dot(xs_ref[...], w_ref[...]) + b_ref[...]
    o_ref[...] = h_ref[...] + gate_ref[...] * acc


def _mm_res_kernel(x_ref, w_ref, b_ref, h_ref, gate_ref, o_ref):
    acc = _dot(x_ref[...], w_ref[...]) + b_ref[...]
    o_ref[...] = h_ref[...] + gate_ref[...] * acc


def _mm_pair(x, wa, ba, wb, bb, tm=512, tn=1024):
    t, k = x.shape
    n = wa.shape[1]
    wspec = pl.BlockSpec((k, tn), lambda i, j: (0, j))
    bspec = pl.BlockSpec((1, tn), lambda i, j: (0, j))
    ospec = pl.BlockSpec((tm, tn), lambda i, j: (i, j))
    return pl.pallas_call(
        _mm_pair_kernel,
        grid=(t // tm, n // tn),
        in_specs=[pl.BlockSpec((tm, k), lambda i, j: (i, 0)), wspec, wspec, bspec, bspec],
        out_specs=[ospec, ospec],
        out_shape=[jax.ShapeDtypeStruct((t, n), F32)] * 2,
        compiler_params=_cparams("parallel", "arbitrary"),
    )(x, wa, wb, ba.reshape(1, n), bb.reshape(1, n))


def _mm_glu(x, w, b, tm=512, tn=1024):
    t, k = x.shape
    n = w.shape[1] // 2
    nj = n // tn
    b2 = b.reshape(1, 2 * n)
    return pl.pallas_call(
        _mm_glu_kernel,
        grid=(t // tm, nj),
        in_specs=[
            pl.BlockSpec((tm, k), lambda i, j: (i, 0)),
            pl.BlockSpec((k, tn), lambda i, j: (0, j)),
            pl.BlockSpec((k, tn), lambda i, j: (0, j + nj)),
            pl.BlockSpec((1, tn), lambda i, j: (0, j)),
            pl.BlockSpec((1, tn), lambda i, j: (0, j + nj)),
        ],
        out_specs=pl.BlockSpec((tm, tn), lambda i, j: (i, j)),
        out_shape=jax.ShapeDtypeStruct((t, n), F32),
        compiler_params=_cparams("parallel", "arbitrary"),
    )(x, w, w, b2, b2)


def _mm_ln_res(x, ln_g, ln_b, w, b, h, mods, gate_index, tm=512, tn=1024):
    t, k = x.shape
    n = w.shape[1]
    return pl.pallas_call(
        _mm_ln_res_kernel,
        grid=(t // tm, n // tn),
        in_specs=[
            pl.BlockSpec((tm, k), lambda i, j: (i, 0)),
            pl.BlockSpec((1, k), lambda i, j: (0, 0)),
            pl.BlockSpec((1, k), lambda i, j: (0, 0)),
            pl.BlockSpec((k, tn), lambda i, j: (0, j)),
            pl.BlockSpec((1, tn), lambda i, j: (0, j)),
            pl.BlockSpec((tm, tn), lambda i, j: (i, j)),
            pl.BlockSpec((None, 1, tn), lambda i, j: (gate_index(i), 0, j)),
        ],
        out_specs=pl.BlockSpec((tm, tn), lambda i, j: (i, j)),
        out_shape=jax.ShapeDtypeStruct((t, n), F32),
        scratch_shapes=[pltpu.VMEM((tm, k), BF16)],
        compiler_params=_cparams("parallel", "arbitrary"),
    )(x, ln_g.reshape(1, k), ln_b.reshape(1, k), w, b.reshape(1, n), h, mods)


def _mm_res(x, w, b, h, mods, gate_index, h_tile_off, tm=512, tn=1024):
    t, k = x.shape
    n = w.shape[1]
    return pl.pallas_call(
        _mm_res_kernel,
        grid=(t // tm, n // tn),
        in_specs=[
            pl.BlockSpec((tm, k), lambda i, j: (i, 0)),
            pl.BlockSpec((k, tn), lambda i, j: (0, j)),
            pl.BlockSpec((1, tn), lambda i, j: (0, j)),
            pl.BlockSpec((tm, tn), lambda i, j: (i + h_tile_off, j)),
            pl.BlockSpec((None, 1, tn), lambda i, j: (gate_index(i), 0, j)),
        ],
        out_specs=pl.BlockSpec((tm, tn), lambda i, j: (i, j)),
        out_shape=jax.ShapeDtypeStruct((t, n), F32),
        compiler_params=_cparams("parallel", "arbitrary"),
    )(x, w, b.reshape(1, n), h, mods)


SEG_PAD = 16


def _conv_seg_kernel(v_ref, w_ref, b_ref, prev_ref, o_ref, pad_ref, *, seg, n_seg):
    del prev_ref
    tc = v_ref.shape[1]
    zeros = jnp.zeros((SEG_PAD, tc), F32)
    for s in range(n_seg):
        pad_ref[s, pl.ds(0, SEG_PAD), :] = zeros
        pad_ref[s, pl.ds(SEG_PAD, seg), :] = v_ref[pl.ds(s * seg, seg), :]
        pad_ref[s, pl.ds(SEG_PAD + seg, SEG_PAD), :] = zeros
    w = w_ref[...]
    bias = b_ref[...]
    chunk = min(seg, 64)
    for s in range(n_seg):
        for c0 in range(0, seg, chunk):
            acc = jnp.zeros((chunk, tc), F32) + bias
            for k in range(CONV_K):
                off = SEG_PAD - CONV_PAD + k + c0
                acc = acc + w[k:k + 1, :] * pad_ref[s, pl.ds(off, chunk), :]
            o_ref[pl.ds(s * seg + c0, chunk), :] = acc


def _conv_seg(v, w, b, prev, *, seg, row0, n_rows, ch0, n_ch, rows_blk=512, tc=128):
    t, d = v.shape
    n_seg = rows_blk // seg
    r_off, c_off = row0 // rows_blk, ch0 // tc
    blk = lambda i, j: (i + r_off, j + c_off)
    return pl.pallas_call(
        functools.partial(_conv_seg_kernel, seg=seg, n_seg=n_seg),
        grid=(n_rows // rows_blk, n_ch // tc),
        in_specs=[
            pl.BlockSpec((rows_blk, tc), blk),
            pl.BlockSpec((CONV_K, tc), lambda i, j: (0, j + c_off)),
            pl.BlockSpec((1, tc), lambda i, j: (0, j + c_off)),
            pl.BlockSpec(memory_space=pl.ANY),
        ],
        out_specs=pl.BlockSpec((rows_blk, tc), blk),
        out_shape=jax.ShapeDtypeStruct((t, d), F32),
        scratch_shapes=[pltpu.VMEM((n_seg, seg + 2 * SEG_PAD, tc), F32)],
        input_output_aliases={3: 0},
        compiler_params=_cparams("parallel", "parallel"),
    )(v, w, b.reshape(1, d), prev)


def _conv_col_kernel(v_ref, w_ref, b_ref, prev_ref, o_ref, pad_ref, *, stride):
    del prev_ref
    n, tc = v_ref.shape
    halo = CONV_PAD * stride
    pad_ref[pl.ds(0, halo), :] = jnp.zeros((halo, tc), F32)
    pad_ref[pl.ds(halo, n), :] = v_ref[...]
    pad_ref[pl.ds(halo + n, halo), :] = jnp.zeros((halo, tc), F32)
    w = w_ref[...]
    bias = b_ref[...]

    def body(c, carry):
        base = pl.multiple_of(c * stride, stride)
        acc = jnp.zeros((stride, tc), F32) + bias
        for k in range(CONV_K):
            acc = acc + w[k:k + 1, :] * pad_ref[pl.ds(base + k * stride, stride), :]
        o_ref[pl.ds(base, stride), :] = acc
        return carry

    lax.fori_loop(0, n // stride, body, 0)


def _conv_col(v, w, b, prev, *, stride, row0, n_col_rows, n_cols, ch0, n_ch, tc=128):
    t, d = v.shape
    r_off, c_off = row0 // n_col_rows, ch0 // tc
    blk = lambda i, j: (i + r_off, j + c_off)
    return pl.pallas_call(
        functools.partial(_conv_col_kernel, stride=stride),
        grid=(n_cols, n_ch // tc),
        in_specs=[
            pl.BlockSpec((n_col_rows, tc), blk),
            pl.BlockSpec((CONV_K, tc), lambda i, j: (0, j + c_off)),
            pl.BlockSpec((1, tc), lambda i, j: (0, j + c_off)),
            pl.BlockSpec(memory_space=pl.ANY),
        ],
        out_specs=pl.BlockSpec((n_col_rows, tc), blk),
        out_shape=jax.ShapeDtypeStruct((t, d), F32),
        scratch_shapes=[pltpu.VMEM((n_col_rows + 2 * CONV_PAD * stride, tc), F32)],
        input_output_aliases={3: 0},
        compiler_params=_cparams("parallel", "parallel"),
    )(v, w, b.reshape(1, d), prev)


def _scan_rows(a, u, carry, reverse):
    n = a.shape[0]
    row = lax.broadcasted_iota(jnp.int32, (SUBLANES, a.shape[1]), 0)
    outs = [None] * (n // SUBLANES)
    order = range(n // SUBLANES - 1, -1, -1) if reverse else range(n // SUBLANES)
    for g in order:
        ag = a[g * SUBLANES:(g + 1) * SUBLANES]
        ug = u[g * SUBLANES:(g + 1) * SUBLANES]
        for s in (1, 2, 4):
            if reverse:
                a_sh = pltpu.roll(ag, SUBLANES - s, 0)
                u_sh = pltpu.roll(ug, SUBLANES - s, 0)
                ok = row < SUBLANES - s
            else:
                a_sh = pltpu.roll(ag, s, 0)
                u_sh = pltpu.roll(ug, s, 0)
                ok = row >= s
            ug = jnp.where(ok, ag * u_sh + ug, ug)
            ag = jnp.where(ok, ag * a_sh, ag)
        hg = ug + ag * carry
        carry = hg[0:1] if reverse else hg[SUBLANES - 1:SUBLANES]
        outs[g] = hg
    return jnp.concatenate(outs, axis=0), carry


def _lru_kernel(p_ref, pprev_ref, pnext_ref, cw_ref, cb_ref, wa_ref, ba_ref, wi_ref, bi_ref,
                lam_ref, *rest, reverse, n_steps, final):
    if final:
        hin_ref, y_ref, o_ref, pad_ref, carry_ref = rest
    else:
        o_ref, pad_ref, carry_ref = rest
    step = pl.program_id(1)
    ts, d = p_ref.shape
    blk = d // LRU_HEADS

    @pl.when(step == 0)
    def _():
        carry_ref[...] = jnp.zeros_like(carry_ref)

    if reverse:
        has_prev = jnp.logical_and(step >= 1, step < n_steps - 1)
        has_next = step >= 2
    else:
        has_prev = step >= 2
        has_next = jnp.logical_and(step >= 1, step < n_steps - 1)
    pad_ref[pl.ds(0, SUBLANES), :] = jnp.where(has_prev, pprev_ref[...], 0.0)
    pad_ref[pl.ds(SUBLANES, ts), :] = p_ref[...]
    pad_ref[pl.ds(SUBLANES + ts, SUBLANES), :] = jnp.where(has_next, pnext_ref[...], 0.0)

    for hd in range(LRU_HEADS):
        cs = slice(hd * blk, (hd + 1) * blk)
        xb = jnp.zeros((ts, blk), F32) + cb_ref[:, cs]
        for k in range(LRU_CONV_K):
            xb = xb + cw_ref[k:k + 1, cs] * pad_ref[pl.ds(SUBLANES - 1 + k, ts), cs]
        xh = xb.astype(BF16)
        r = _sigmoid(_dot(xh, wa_ref[hd]) + ba_ref[:, cs])
        ig = _sigmoid(_dot(xh, wi_ref[hd]) + bi_ref[:, cs])
        z = -lam_ref[:, cs]
        softplus = jnp.maximum(z, 0.0) + jnp.log(1.0 + jnp.exp(-jnp.abs(z)))
        log_a = -LRU_C * r * softplus
        a = jnp.exp(log_a)
        mult = jnp.sqrt(1.0 - jnp.exp(2.0 * log_a))
        u = xb * ig * mult
        hs, carry = _scan_rows(a, u, carry_ref[:, cs], reverse)
        carry_ref[:, cs] = carry
        if final:
            o_ref[:, cs] = (y_ref[:, cs] * (hin_ref[:, cs] + hs)).astype(o_ref.dtype)
        else:
            o_ref[:, cs] = hs


def _lru_direction(p, cw, cb, wa, ba, wi, bi, lam, *, reverse, n_batch, ctx_len, seq, h_in=None, y=None):
    t, d = p.shape
    ts = ROW_TILE
    assert ctx_len == ts
    lat_tiles = seq // ts
    n_steps = lat_tiles + 1
    ctx_tiles = n_batch
    sub_per_tile = ts // SUBLANES
    n_sub = t // SUBLANES
    final = h_in is not None

    def lat_idx(s):
        return (lat_tiles - s) if reverse else (s - 1)

    def tile(b, s):
        return jnp.where(s == 0, n_batch * lat_tiles + b, out_tile(b, s))

    def out_tile(b, s):
        return b * lat_tiles + lat_idx(jnp.maximum(s, 1))

    p_spec = pl.BlockSpec((ts, d), lambda b, s: (tile(b, s), 0))
    prev_spec = pl.BlockSpec((SUBLANES, d), lambda b, s: (jnp.maximum(tile(b, s) * sub_per_tile - 1, 0), 0))
    next_spec = pl.BlockSpec((SUBLANES, d), lambda b, s: (jnp.minimum((tile(b, s) + 1) * sub_per_tile, n_sub - 1), 0))
    vec = pl.BlockSpec((1, d), lambda b, s: (0, 0))
    blk = d // LRU_HEADS
    wspec = pl.BlockSpec((LRU_HEADS, blk, blk), lambda b, s: (0, 0, 0))
    out_spec = pl.BlockSpec((ts, d), lambda b, s: (out_tile(b, s), 0))
    in_specs = [p_spec, prev_spec, next_spec,
                pl.BlockSpec((LRU_CONV_K, d), lambda b, s: (0, 0)), vec,
                wspec, vec, wspec, vec, vec]
    args = [p, p, p, cw, cb.reshape(1, d), wa, ba.reshape(1, d), wi, bi.reshape(1, d), lam.reshape(1, d)]
    if final:
        in_specs += [out_spec, out_spec]
        args += [h_in, y]
    return pl.pallas_call(
        functools.partial(_lru_kernel, reverse=reverse, n_steps=n_steps, final=final),
        grid=(n_batch, n_steps),
        in_specs=in_specs,
        out_specs=out_spec,
        out_shape=jax.ShapeDtypeStruct((n_batch * seq, d), BF16 if final else F32),
        scratch_shapes=[pltpu.VMEM((ts + 2 * SUBLANES, d), F32), pltpu.VMEM((1, d), F32)],
        compiler_params=_cparams("arbitrary", "arbitrary"),
    )(*args)


def _router_kernel(h_ref, g_ref, sh_ref, sc_ref, rw_ref, rb_ref,
                   fn_ref, idx_ref, prob_ref, rank_ref, cnt_ref, carry_ref):
    i = pl.program_id(0)
    tm = h_ref.shape[0]
    n_e = rw_ref.shape[0]

    @pl.when(i == 0)
    def _():
        carry_ref[...] = jnp.zeros_like(carry_ref)

    fn = _rms_mod(h_ref[...], g_ref[...], sh_ref[...], sc_ref[...])
    fn_ref[...] = fn

    f_hi, f_lo = _split_bf16(fn)
    w_hi, w_lo = _split_bf16(rw_ref[...])
    logits = _dot_nt(w_hi, f_hi) + _dot_nt(w_lo, f_hi) + _dot_nt(w_hi, f_lo) + rb_ref[...]

    e_iota = lax.broadcasted_iota(jnp.int32, (n_e, tm), 0).astype(F32)
    taken = jnp.zeros((n_e, tm), F32)
    neg_inf = jnp.float32(-jnp.inf)
    sels, vals, idxs = [], [], []
    for _ in range(TOP_K):
        masked = jnp.where(taken > 0.0, neg_inf, logits)
        m = jnp.max(masked, axis=0, keepdims=True)
        cand = jnp.logical_and(masked == m, taken == 0.0)
        idx = jnp.min(jnp.where(cand, e_iota, float(n_e)), axis=0, keepdims=True)
        sel = e_iota == idx
        taken = jnp.where(sel, 1.0, taken)
        sels.append(sel)
        vals.append(m)
        idxs.append(idx)
    exps = [jnp.exp(v - vals[0]) for v in vals]
    denom = exps[0] + exps[1] + exps[2] + exps[3]
    onehot = jnp.zeros((n_e, tm), F32)
    for sel in sels:
        onehot = onehot + jnp.where(sel, 1.0, 0.0)

    r_iota = lax.broadcasted_iota(jnp.int32, (tm, tm), 0)
    c_iota = lax.broadcasted_iota(jnp.int32, (tm, tm), 1)
    upper = jnp.where(r_iota < c_iota, 1.0, 0.0).astype(BF16)
    before = _dot(onehot.astype(BF16), upper) + carry_ref[:, 0:1]
    for k in range(TOP_K):
        idx_ref[pl.ds(k, 1), :] = idxs[k].astype(jnp.int32)
        prob_ref[pl.ds(k, 1), :] = exps[k] / denom
        rank_ref[pl.ds(k, 1), :] = jnp.sum(jnp.where(sels[k], before, 0.0), axis=0,
                                           keepdims=True).astype(jnp.int32)
    carry_ref[...] = carry_ref[...] + jnp.sum(onehot, axis=1, keepdims=True)
    cnt_ref[...] = carry_ref[...]


def _router(h, g, mods, sh_index, sc_index, rw_t, rb, n_tiles, tile_off):
    t, d = h.shape
    n_e = rw_t.shape[0]
    tm = ROW_TILE
    n_tok = n_tiles * tm
    small = lambda dt: jax.ShapeDtypeStruct((TOP_K, n_tok), dt)
    small_spec = pl.BlockSpec((TOP_K, tm), lambda i: (0, i))
    return pl.pallas_call(
        _router_kernel,
        grid=(n_tiles,),
        in_specs=[
            pl.BlockSpec((tm, d), lambda i: (i + tile_off, 0)),
            pl.BlockSpec((1, d), lambda i: (0, 0)),
            _mod_spec(d, sh_index),
            _mod_spec(d, sc_index),
            pl.BlockSpec((n_e, d), lambda i: (0, 0)),
            pl.BlockSpec((n_e, 1), lambda i: (0, 0)),
        ],
        out_specs=[
            pl.BlockSpec((tm, d), lambda i: (i, 0)),
            small_spec, small_spec, small_spec,
            pl.BlockSpec((n_e, LANES), lambda i: (0, 0)),
        ],
        out_shape=[
            jax.ShapeDtypeStruct((n_tok, d), F32),
            small(jnp.int32), small(F32), small(jnp.int32),
            jax.ShapeDtypeStruct((n_e, LANES), F32),
        ],
        scratch_shapes=[pltpu.VMEM((n_e, LANES), F32)],
        compiler_params=_cparams("arbitrary"),
    )(h, g.reshape(1, d), mods, mods, rw_t, rb.reshape(n_e, 1))


def _slot_stride(n_tok):
    return 1 << (n_tok - 1).bit_length()


def _slot_map_kernel(pos_ref, o_ref, *, stride):
    i = pl.program_id(0)
    tm = pos_ref.shape[2]
    n_sorted = o_ref.shape[0]

    @pl.when(i == 0)
    def _():
        def init(s, c):
            o_ref[s] = TOP_K * stride + s
            return c
        lax.fori_loop(0, n_sorted, init, 0)

    def body(r, c):
        for k in range(TOP_K):
            o_ref[pos_ref[0, k, r]] = k * stride + i * tm + r
        return c

    lax.fori_loop(0, tm, body, 0)


def _slot_map(pos_tiles, n_sorted):
    n_tiles, _, tm = pos_tiles.shape
    return pl.pallas_call(
        functools.partial(_slot_map_kernel, stride=_slot_stride(n_tiles * tm)),
        grid=(n_tiles,),
        in_specs=[pl.BlockSpec((1, TOP_K, tm), lambda i: (i, 0, 0), memory_space=pltpu.SMEM)],
        out_specs=pl.BlockSpec(memory_space=pltpu.SMEM),
        out_shape=jax.ShapeDtypeStruct((n_sorted,), jnp.int32),
        compiler_params=_cparams("arbitrary"),
    )(pos_tiles)


def _combine_kernel(y0_ref, y1_ref, y2_ref, y3_ref, p_ref, h_ref, gate_ref, fg_ref, o_ref, *, final_norm):
    p = p_ref[...]
    acc = p[:, 0:1] * y0_ref[...]
    for k, y_ref in ((1, y1_ref), (2, y2_ref), (3, y3_ref)):
        acc = acc + p[:, k:k + 1] * y_ref[...]
    out = h_ref[...] + gate_ref[...] * acc
    if final_norm:
        ms = jnp.mean(out * out, axis=-1, keepdims=True)
        out = out * lax.rsqrt(ms + EPS) * fg_ref[...]
    o_ref[...] = out


def _combine(y, probs, h, mods, gate_index, final_g, final_norm, tm=ROW_TILE):
    assert TOP_K == 4
    n_tok = probs.shape[0]
    d = y.shape[1]
    k_blk = _slot_stride(n_tok) // tm
    y_specs = [pl.BlockSpec((tm, d), functools.partial(lambda i, k: (k * k_blk + i, 0), k=k)) for k in range(TOP_K)]
    return pl.pallas_call(
        functools.partial(_combine_kernel, final_norm=final_norm),
        grid=(n_tok // tm,),
        in_specs=y_specs + [
            pl.BlockSpec((tm, TOP_K), lambda i: (i, 0)),
            pl.BlockSpec((tm, d), lambda i: (i, 0)),
            _mod_spec(d, gate_index),
            pl.BlockSpec((1, d), lambda i: (0, 0)),
        ],
        out_specs=pl.BlockSpec((tm, d), lambda i: (i, 0)),
        out_shape=jax.ShapeDtypeStruct((n_tok, d), F32),
        compiler_params=_cparams("parallel"),
    )(y, y, y, y, probs, h, mods, final_g.reshape(1, d))


W_CHUNK = 512
W_RING = 3


def _experts_kernel(te_ref, tv_ref, slot_ref, nxt_ref, c0_ref, c1_ref,
                    next_map_ref, prev_map_ref, b1_ref, b2_ref, fn_hbm, w1_hbm, w2_hbm, y_hbm,
                    wbf_ref, ring_ref, xin_ref, xbf_ref, act_ref, yacc_ref, yout_ref, sems, xsem, ysem,
                    *, layer, n1, n_chunks, n_tok):
    i = pl.program_id(0)
    e_cur = te_ref[i]
    slot = slot_ref[i]
    nxt = nxt_ref[i]
    tm, d = xin_ref.shape
    f = act_ref.shape[1]
    valid = tv_ref[i] > 0

    def gather_copy(map_ref, r):
        tok = jnp.minimum(map_ref[0, 0, r] & (_slot_stride(n_tok) - 1), n_tok - 1)
        return pltpu.make_async_copy(fn_hbm.at[pl.ds(tok, 1)], xin_ref.at[pl.ds(r, 1)], xsem)

    def scatter_copy(r):
        return pltpu.make_async_copy(yout_ref.at[pl.ds(r, 1)], y_hbm.at[pl.ds(prev_map_ref[0, 0, r], 1)], ysem)

    def start_gathers(map_ref):
        for r in range(tm):
            gather_copy(map_ref, r).start()

    def wait_gathers():
        for r in range(tm):
            pltpu.make_async_copy(fn_hbm.at[pl.ds(0, 1)], xin_ref.at[pl.ds(r, 1)], xsem).wait()

    def start_scatters():
        for r in range(tm):
            scatter_copy(r).start()

    def wait_scatters():
        for r in range(tm):
            pltpu.make_async_copy(yout_ref.at[pl.ds(r, 1)], y_hbm.at[pl.ds(0, 1)], ysem).wait()

    def start_chunk(e, c):
        r = lax.rem(c, W_RING)

        @pl.when(c < n1)
        def _():
            pltpu.make_async_copy(w1_hbm.at[layer, e, pl.ds(c * W_CHUNK, W_CHUNK)],
                                  ring_ref.at[r], sems.at[r]).start()

        @pl.when(c >= n1)
        def _():
            pltpu.make_async_copy(w2_hbm.at[layer, e, pl.ds((c - n1) * W_CHUNK, W_CHUNK)],
                                  ring_ref.at[r], sems.at[r]).start()

    def land_chunks(e, dst_slot, c_lo, c_hi):
        def body(c, carry):
            r = lax.rem(c, W_RING)
            pltpu.make_async_copy(w1_hbm.at[layer, 0, pl.ds(0, W_CHUNK)], ring_ref.at[r], sems.at[r]).wait()
            wbf_ref[dst_slot, pl.ds(pl.multiple_of(c * W_CHUNK, W_CHUNK), W_CHUNK), :] = ring_ref[r].astype(BF16)

            @pl.when(c + W_RING < n_chunks)
            def _():
                start_chunk(e, c + W_RING)
            return carry

        lax.fori_loop(c_lo, c_hi, body, 0)

    @pl.when(i == 0)
    def _():
        for c in range(W_RING):
            start_chunk(e_cur, jnp.int32(c))
        land_chunks(e_cur, slot, 0, n_chunks)
        yout_ref[...] = jnp.zeros_like(yout_ref)
        start_gathers(prev_map_ref)

    first = jnp.logical_or(i == 0, e_cur != te_ref[jnp.maximum(i - 1, 0)])

    @pl.when(valid)
    def _():
        wait_gathers()
        xbf_ref[...] = xin_ref[...].astype(BF16)

        @pl.when(jnp.logical_and(first, nxt >= 0))
        def _():
            for c in range(W_RING):
                start_chunk(nxt, jnp.int32(c))

        start_gathers(next_map_ref)
        start_scatters()
        tn = 512
        for j in range(f // tn):
            ca = slice(j * tn, (j + 1) * tn)
            cb = slice(f + j * tn, f + (j + 1) * tn)
            glu = _dot(xbf_ref[...], wbf_ref[slot, 0:d, ca]) + b1_ref[0, :, ca]
            lin = _dot(xbf_ref[...], wbf_ref[slot, 0:d, cb]) + b1_ref[0, :, cb]
            glu = jnp.minimum(glu, SWIGLU_LIMIT)
            lin = jnp.clip(lin, -SWIGLU_LIMIT, SWIGLU_LIMIT)
            act_ref[:, ca] = (glu * _sigmoid(SWIGLU_ALPHA * glu) * (lin + 1.0)).astype(BF16)
        yacc_ref[...] = _dot(act_ref[...], wbf_ref[slot, d:d + f, :]) + b2_ref[0]

        @pl.when(nxt >= 0)
        def _():
            land_chunks(nxt, 1 - slot, c0_ref[i], c1_ref[i])

        wait_scatters()
        yout_ref[...] = yacc_ref[...]

    @pl.when(jnp.logical_and(jnp.logical_not(valid), tv_ref[jnp.maximum(i - 1, 0)] > 0))
    def _():
        wait_gathers()
        start_scatters()
        wait_scatters()


def _experts(fn, slot_map, sched, w1, b1, w2, b2, layer):
    n_tok, d = fn.shape
    n_sorted = slot_map.shape[0]
    _, n_e, _, two_f = w1.shape
    f = two_f // 2
    assert d == two_f and d % W_CHUNK == 0 and f % W_CHUNK == 0
    n1, n_chunks = d // W_CHUNK, (d + f) // W_CHUNK
    tm = ROW_TILE
    n_tiles = n_sorted // tm
    maps = slot_map.reshape(n_tiles, 1, tm)
    grid_spec = pltpu.PrefetchScalarGridSpec(
        num_scalar_prefetch=6,
        grid=(n_tiles,),
        in_specs=[
            pl.BlockSpec((1, 1, tm), lambda i, *_: (jnp.minimum(i + 1, n_tiles - 1), 0, 0), memory_space=pltpu.SMEM),
            pl.BlockSpec((1, 1, tm), lambda i, *_: (jnp.maximum(i - 1, 0), 0, 0), memory_space=pltpu.SMEM),
            pl.BlockSpec((1, 1, two_f), lambda i, te, *_: (layer * n_e + te[i], 0, 0)),
            pl.BlockSpec((1, 1, d), lambda i, te, *_: (layer * n_e + te[i], 0, 0)),
            pl.BlockSpec(memory_space=pl.ANY),
            pl.BlockSpec(memory_space=pl.ANY),
            pl.BlockSpec(memory_space=pl.ANY),
        ],
        out_specs=pl.BlockSpec(memory_space=pl.ANY),
        scratch_shapes=[
            pltpu.VMEM((2, d + f, d), BF16),
            pltpu.VMEM((W_RING, W_CHUNK, d), F32),
            pltpu.VMEM((tm, d), F32),
            pltpu.VMEM((tm, d), BF16),
            pltpu.VMEM((tm, f), BF16),
            pltpu.VMEM((tm, d), F32),
            pltpu.VMEM((tm, d), F32),
            pltpu.SemaphoreType.DMA((W_RING,)),
            pltpu.SemaphoreType.DMA(()),
            pltpu.SemaphoreType.DMA(()),
        ],
    )
    return pl.pallas_call(
        functools.partial(_experts_kernel, layer=layer, n1=n1, n_chunks=n_chunks, n_tok=n_tok),
        grid_spec=grid_spec,
        out_shape=jax.ShapeDtypeStruct((TOP_K * _slot_stride(n_tok) + n_sorted, d), F32),
        compiler_params=_cparams("arbitrary"),
    )(*sched, maps, maps, b1.reshape(-1, 1, two_f), b2.reshape(-1, 1, d), fn, w1, w2)


def _moe(h, g, mods, mod_index, router_w, router_b, w1, b1, w2, b2, layer, final_g, n_tiles, final_norm):
    tm = ROW_TILE
    n_e = router_w.shape[1]
    n_tok = n_tiles * tm
    fn, idx, prob, rank, cnt = _router(h, g, mods, mod_index(3, tm), mod_index(4, tm), router_w.T, router_b,
                                       n_tiles, 0)

    i32 = jnp.int32
    counts = cnt[:, 0].astype(i32)
    e_tiles = (counts + tm - 1) // tm
    e_end = jnp.cumsum(e_tiles)
    e_start = e_end - e_tiles
    experts = jnp.arange(n_e, dtype=i32)
    onehot = idx[None] == experts[:, None, None]
    pos = jnp.sum(jnp.where(onehot, e_start[:, None, None] * tm, 0), axis=0) + rank
    n_tiles_sorted = (n_tok * TOP_K) // tm + n_e
    tiles = jnp.arange(n_tiles_sorted, dtype=i32)
    tile_valid = (tiles < e_end[-1]).astype(i32)
    last_tile = jnp.maximum(e_end[-1] - 1, 0)
    tile_expert = jnp.sum((e_end[None, :] <= jnp.minimum(tiles, last_tile)[:, None]).astype(i32), axis=1)
    tile_expert = jnp.minimum(tile_expert, n_e - 1)
    nonempty = e_tiles > 0
    later = jnp.where(nonempty[None, :] & (experts[None, :] > experts[:, None]), experts[None, :], n_e)
    e_next = jnp.min(later, axis=1)
    e_next = jnp.where(e_next < n_e, e_next, -1)
    e_slot = (jnp.cumsum(nonempty.astype(i32)) - nonempty.astype(i32)) % 2
    n_chunks = (w1.shape[2] + w2.shape[2]) // W_CHUNK
    local = tiles - e_start[tile_expert]
    per = jnp.maximum(e_tiles[tile_expert], 1)
    tile_next = jnp.where(tile_valid > 0, e_next[tile_expert], -1)
    has_next = tile_next >= 0
    c_lo = jnp.where(has_next, (n_chunks * local) // per, 0)
    c_hi = jnp.where(has_next, (n_chunks * (local + 1)) // per, 0)
    sched = (tile_expert, tile_valid, e_slot[tile_expert], tile_next, c_lo, c_hi)

    pos_tiles = pos.reshape(TOP_K, n_tiles, tm).transpose(1, 0, 2)
    slot_map = _slot_map(pos_tiles, n_tiles_sorted * tm)
    y = _experts(fn, slot_map, sched, w1, b1, w2, b2, layer)
    return _combine(y, prob.T, h, mods, mod_index(5, tm), final_g, final_norm)


def kernel(x, c, ctx, c_ctx, ada_w, ada_b, norm1_g, norm2_g, conv_pw1_w, conv_pw1_b, conv_dw_w, conv_dw_b, conv_ln_g, conv_ln_b, conv_pw2_w, conv_pw2_b, lru_wy, lru_by, lru_wx, lru_bx, lru_conv_w, lru_conv_b, lru_wa, lru_ba, lru_wi, lru_bi, lru_lambda, lru_wo, lru_bo, router_w, router_b, moe_w1, moe_b1, moe_w2, moe_b2, final_g):
    n_batch, seq, d = x.shape
    ctx_len = ctx.shape[1]
    depth = ada_w.shape[0]
    tm = ROW_TILE
    n_ctx, n_lat = n_batch * ctx_len, n_batch * seq
    t_all = n_ctx + n_lat
    lat_tiles, all_tiles = n_lat // tm, t_all // tm
    ctx_src = n_batch
    n_src = n_batch + 1
    big = 512
    assert depth == 2 and ctx_len == tm and seq % big == 0 and n_ctx % big == 0 and n_src <= SUBLANES

    src = jnp.zeros((SUBLANES, d), F32).at[:n_batch].set(c).at[ctx_src].set(c_ctx)
    mods = _ada(src, ada_w, ada_b)[:, :n_src].reshape(depth * n_src * N_ADA, 1, d)

    def mod_index(layer):
        def for_chunk(chunk, rows):
            def index(i):
                row = i * rows
                tile_src = jnp.where(row >= n_lat, ctx_src, row // seq)
                return (layer * n_src + tile_src) * N_ADA + chunk
            return index
        return for_chunk

    bf = lambda a: a.astype(BF16)
    h = jnp.concatenate([x.reshape(n_lat, d), ctx.reshape(n_ctx, d)], axis=0)

    mi = mod_index(0)
    xn = _norm_mod(h, norm1_g[0], mods, mi(0, tm), mi(1, tm), all_tiles)
    v = _mm_glu(xn, bf(conv_pw1_w[0]), conv_pw1_b[0], tm=big)
    half = d // 2
    dw_w, dw_b = conv_dw_w[0], conv_dw_b[0]
    cv = jnp.zeros((t_all, d), F32)
    cv = _conv_seg(v, dw_w, dw_b, cv, seg=ctx_len, row0=n_lat, n_rows=n_ctx, ch0=0, n_ch=d, rows_blk=ctx_len)
    cv = _conv_seg(v, dw_w, dw_b, cv, seg=GRID_W, row0=0, n_rows=n_lat, ch0=0, n_ch=half)
    cv = _conv_col(v, dw_w, dw_b, cv, stride=GRID_W, row0=0, n_col_rows=seq, n_cols=n_batch, ch0=half, n_ch=half)
    h = _mm_ln_res(cv, conv_ln_g[0], conv_ln_b[0], bf(conv_pw2_w[0]), conv_pw2_b[0], h, mods, mi(2, big), tm=big)
    h = _moe(h, norm2_g[0], mods, mi, router_w[0], router_b[0], moe_w1, moe_b1, moe_w2, moe_b2, 0,
             final_g, all_tiles, False)

    mi = mod_index(1)
    xn = _norm_mod(h, norm1_g[1], mods, mi(0, tm), mi(1, tm), all_tiles)
    p, yl = _mm_pair(xn, bf(lru_wx[0]), lru_bx[0], bf(lru_wy[0]), lru_by[0], tm=big)
    common = dict(n_batch=n_batch, ctx_len=ctx_len, seq=seq)
    hf = _lru_direction(p, lru_conv_w[0], lru_conv_b[0], bf(lru_wa[0, 0]), lru_ba[0, 0], bf(lru_wi[0, 0]),
                        lru_bi[0, 0], lru_lambda[0, 0], reverse=False, **common)
    z = _lru_direction(p, lru_conv_w[0], lru_conv_b[0], bf(lru_wa[0, 1]), lru_ba[0, 1], bf(lru_wi[0, 1]),
                       lru_bi[0, 1], lru_lambda[0, 1], reverse=True, h_in=hf, y=yl, **common)
    h_lat = _mm_res(z, bf(lru_wo[0]), lru_bo[0], h, mods, mi(2, big), 0, tm=big)
    out = _moe(h_lat, norm2_g[1], mods, mi, router_w[1], router_b[1], moe_w1, moe_b1, moe_w2, moe_b2, 1,
               final_g, lat_tiles, True)
    return out.reshape(n_batch, seq, d)
```

```python
import functools

import jax
import jax.numpy as jnp
from jax import lax
from jax.experimental import pallas as pl
from jax.experimental.pallas import tpu as pltpu

F32 = jnp.float32
BF16 = jnp.bfloat16

GRID_W = 64
N_ADA = 6
EPS = 1e-6
CONV_K = 31
CONV_PAD = CONV_K // 2
LRU_HEADS = 16
LRU_C = 8.0
LRU_CONV_K = 4
TOP_K = 4
SWIGLU_ALPHA = 1.702
SWIGLU_LIMIT = 7.0

ROW_TILE = 256
SUBLANES = 8
LANES = 128
VMEM_LIMIT = 56 * 1024 * 1024


def _cparams(*sem):
    return pltpu.CompilerParams(dimension_semantics=sem, vmem_limit_bytes=VMEM_LIMIT)


def _sigmoid(x):
    return 1.0 / (1.0 + jnp.exp(-x))


def _split_bf16(a):
    hi = a.astype(BF16)
    lo = (a - hi.astype(F32)).astype(BF16)
    return hi, lo


def _dot(a, b):
    return jnp.dot(a, b, preferred_element_type=F32)


def _dot_nt(a, b):
    return lax.dot_general(a, b, (((1,), (1,)), ((), ())), preferred_element_type=F32)


def _ada_kernel(s_ref, w_ref, b_ref, o_ref):
    s = s_ref[...]
    s = s * _sigmoid(s)
    s_hi, s_lo = _split_bf16(s)
    w_hi, w_lo = _split_bf16(w_ref[0])
    acc = _dot(s_hi, w_hi) + _dot(s_hi, w_lo) + _dot(s_lo, w_hi)
    o_ref[0] = acc + b_ref[0]


def _ada(src, ada_w, ada_b, tn=1024):
    depth, d, n = ada_w.shape
    return pl.pallas_call(
        _ada_kernel,
        grid=(depth, n // tn),
        in_specs=[
            pl.BlockSpec((SUBLANES, d), lambda l, j: (0, 0)),
            pl.BlockSpec((1, d, tn), lambda l, j: (l, 0, j)),
            pl.BlockSpec((1, 1, tn), lambda l, j: (l, 0, j)),
        ],
        out_specs=pl.BlockSpec((1, SUBLANES, tn), lambda l, j: (l, 0, j)),
        out_shape=jax.ShapeDtypeStruct((depth, SUBLANES, n), F32),
        compiler_params=_cparams("parallel", "parallel"),
    )(src, ada_w, ada_b.reshape(depth, 1, n))


def _rms_mod(x, g, sh, sc):
    ms = jnp.mean(x * x, axis=-1, keepdims=True)
    y = x * lax.rsqrt(ms + EPS) * g
    return y * (1.0 + sc) + sh


def _norm_mod_kernel(h_ref, g_ref, sh_ref, sc_ref, o_ref):
    o_ref[...] = _rms_mod(h_ref[...], g_ref[...], sh_ref[...], sc_ref[...]).astype(o_ref.dtype)


def _mod_spec(d, mod_index):
    return pl.BlockSpec((None, 1, d), lambda i, *_: (mod_index(i), 0, 0))


def _norm_mod(h, g, mods, sh_index, sc_index, n_tiles, tile_off=0):
    t, d = h.shape
    return pl.pallas_call(
        _norm_mod_kernel,
        grid=(n_tiles,),
        in_specs=[
            pl.BlockSpec((ROW_TILE, d), lambda i: (i + tile_off, 0)),
            pl.BlockSpec((1, d), lambda i: (0, 0)),
            _mod_spec(d, sh_index),
            _mod_spec(d, sc_index),
        ],
        out_specs=pl.BlockSpec((ROW_TILE, d), lambda i: (i, 0)),
        out_shape=jax.ShapeDtypeStruct((n_tiles * ROW_TILE, d), BF16),
        compiler_params=_cparams("parallel"),
    )(h, g.reshape(1, d), mods, mods)


def _layernorm_silu(x, g, b):
    mu = jnp.mean(x, axis=-1, keepdims=True)
    xc = x - mu
    y = xc * lax.rsqrt(jnp.mean(xc * xc, axis=-1, keepdims=True) + EPS)
    y = y * g + b
    return y * _sigmoid(y)


def _gelu_tanh(x):
    return 0.5 * x * (1.0 + jnp.tanh(0.7978845608028654 * (x + 0.044715 * (x * x * x))))


def _mm_pair_kernel(x_ref, wa_ref, wb_ref, ba_ref, bb_ref, oa_ref, ob_ref):
    x = x_ref[...]
    oa_ref[...] = _dot(x, wa_ref[...]) + ba_ref[...]
    ob_ref[...] = _gelu_tanh(_dot(x, wb_ref[...]) + bb_ref[...])


def _mm_glu_kernel(x_ref, wa_ref, wb_ref, ba_ref, bb_ref, o_ref):
    x = x_ref[...]
    ua = _dot(x, wa_ref[...]) + ba_ref[...]
    ub = _dot(x, wb_ref[...]) + bb_ref[...]
    o_ref[...] = ua * _sigmoid(ub)


def _mm_ln_res_kernel(x_ref, lg_ref, lb_ref, w_ref, b_ref, h_ref, gate_ref, o_ref, xs_ref):
    @pl.when(pl.program_id(1) == 0)
    def _():
        xs_ref[...] = _layernorm_silu(x_ref[...], lg_ref[...], lb_ref[...]).astype(BF16)

    acc = _dot(xs_ref[...], w_ref[...]) + b_ref[...]
    o_ref[...] = h_ref[...] + gate_ref[...] * acc


def _mm_res_kernel(x_ref, w_ref, b_ref, h_ref, gate_ref, o_ref):
    acc = _dot(x_ref[...], w_ref[...]) + b_ref[...]
    o_ref[...] = h_ref[...] + gate_ref[...] * acc


def _mm_pair(x, wa, ba, wb, bb, tm=512, tn=1024):
    t, k = x.shape
    n = wa.shape[1]
    wspec = pl.BlockSpec((k, tn), lambda i, j: (0, j))
    bspec = pl.BlockSpec((1, tn), lambda i, j: (0, j))
    ospec = pl.BlockSpec((tm, tn), lambda i, j: (i, j))
    return pl.pallas_call(
        _mm_pair_kernel,
        grid=(t // tm, n // tn),
        in_specs=[pl.BlockSpec((tm, k), lambda i, j: (i, 0)), wspec, wspec, bspec, bspec],
        out_specs=[ospec, ospec],
        out_shape=[jax.ShapeDtypeStruct((t, n), F32)] * 2,
        compiler_params=_cparams("parallel", "arbitrary"),
    )(x, wa, wb, ba.reshape(1, n), bb.reshape(1, n))


def _mm_glu(x, w, b, tm=512, tn=1024):
    t, k = x.shape
    n = w.shape[1] // 2
    nj = n // tn
    b2 = b.reshape(1, 2 * n)
    return pl.pallas_call(
        _mm_glu_kernel,
        grid=(t // tm, nj),
        in_specs=[
            pl.BlockSpec((tm, k), lambda i, j: (i, 0)),
            pl.BlockSpec((k, tn), lambda i, j: (0, j)),
            pl.BlockSpec((k, tn), lambda i, j: (0, j + nj)),
            pl.BlockSpec((1, tn), lambda i, j: (0, j)),
            pl.BlockSpec((1, tn), lambda i, j: (0, j + nj)),
        ],
        out_specs=pl.BlockSpec((tm, tn), lambda i, j: (i, j)),
        out_shape=jax.ShapeDtypeStruct((t, n), F32),
        compiler_params=_cparams("parallel", "arbitrary"),
    )(x, w, w, b2, b2)


def _mm_ln_res(x, ln_g, ln_b, w, b, h, mods, gate_index, tm=512, tn=1024):
    t, k = x.shape
    n = w.shape[1]
    return pl.pallas_call(
        _mm_ln_res_kernel,
        grid=(t // tm, n // tn),
        in_specs=[
            pl.BlockSpec((tm, k), lambda i, j: (i, 0)),
            pl.BlockSpec((1, k), lambda i, j: (0, 0)),
            pl.BlockSpec((1, k), lambda i, j: (0, 0)),
            pl.BlockSpec((k, tn), lambda i, j: (0, j)),
            pl.BlockSpec((1, tn), lambda i, j: (0, j)),
            pl.BlockSpec((tm, tn), lambda i, j: (i, j)),
            pl.BlockSpec((None, 1, tn), lambda i, j: (gate_index(i), 0, j)),
        ],
        out_specs=pl.BlockSpec((tm, tn), lambda i, j: (i, j)),
        out_shape=jax.ShapeDtypeStruct((t, n), F32),
        scratch_shapes=[pltpu.VMEM((tm, k), BF16)],
        compiler_params=_cparams("parallel", "arbitrary"),
    )(x, ln_g.reshape(1, k), ln_b.reshape(1, k), w, b.reshape(1, n), h, mods)


def _mm_res(x, w, b, h, mods, gate_index, h_tile_off, tm=512, tn=1024):
    t, k = x.shape
    n = w.shape[1]
    return pl.pallas_call(
        _mm_res_kernel,
        grid=(t // tm, n // tn),
        in_specs=[
            pl.BlockSpec((tm, k), lambda i, j: (i, 0)),
            pl.BlockSpec((k, tn), lambda i, j: (0, j)),
            pl.BlockSpec((1, tn), lambda i, j: (0, j)),
            pl.BlockSpec((tm, tn), lambda i, j: (i + h_tile_off, j)),
            pl.BlockSpec((None, 1, tn), lambda i, j: (gate_index(i), 0, j)),
        ],
        out_specs=pl.BlockSpec((tm, tn), lambda i, j: (i, j)),
        out_shape=jax.ShapeDtypeStruct((t, n), F32),
        compiler_params=_cparams("parallel", "arbitrary"),
    )(x, w, b.reshape(1, n), h, mods)


SEG_PAD = 16


def _conv_seg_kernel(v_ref, w_ref, b_ref, prev_ref, o_ref, pad_ref, *, seg, n_seg):
    del prev_ref
    tc = v_ref.shape[1]
    zeros = jnp.zeros((SEG_PAD, tc), F32)
    for s in range(n_seg):
        pad_ref[s, pl.ds(0, SEG_PAD), :] = zeros
        pad_ref[s, pl.ds(SEG_PAD, seg), :] = v_ref[pl.ds(s * seg, seg), :]
        pad_ref[s, pl.ds(SEG_PAD + seg, SEG_PAD), :] = zeros
    w = w_ref[...]
    bias = b_ref[...]
    chunk = min(seg, 64)
    for s in range(n_seg):
        for c0 in range(0, seg, chunk):
            acc = jnp.zeros((chunk, tc), F32) + bias
            for k in range(CONV_K):
                off = SEG_PAD - CONV_PAD + k + c0
                acc = acc + w[k:k + 1, :] * pad_ref[s, pl.ds(off, chunk), :]
            o_ref[pl.ds(s * seg + c0, chunk), :] = acc


def _conv_seg(v, w, b, prev, *, seg, row0, n_rows, ch0, n_ch, rows_blk=512, tc=128):
    t, d = v.shape
    n_seg = rows_blk // seg
    r_off, c_off = row0 // rows_blk, ch0 // tc
    blk = lambda i, j: (i + r_off, j + c_off)
    return pl.pallas_call(
        functools.partial(_conv_seg_kernel, seg=seg, n_seg=n_seg),
        grid=(n_rows // rows_blk, n_ch // tc),
        in_specs=[
            pl.BlockSpec((rows_blk, tc), blk),
            pl.BlockSpec((CONV_K, tc), lambda i, j: (0, j + c_off)),
            pl.BlockSpec((1, tc), lambda i, j: (0, j + c_off)),
            pl.BlockSpec(memory_space=pl.ANY),
        ],
        out_specs=pl.BlockSpec((rows_blk, tc), blk),
        out_shape=jax.ShapeDtypeStruct((t, d), F32),
        scratch_shapes=[pltpu.VMEM((n_seg, seg + 2 * SEG_PAD, tc), F32)],
        input_output_aliases={3: 0},
        compiler_params=_cparams("parallel", "parallel"),
    )(v, w, b.reshape(1, d), prev)


def _conv_col_kernel(v_ref, w_ref, b_ref, prev_ref, o_ref, pad_ref, *, stride):
    del prev_ref
    n, tc = v_ref.shape
    halo = CONV_PAD * stride
    pad_ref[pl.ds(0, halo), :] = jnp.zeros((halo, tc), F32)
    pad_ref[pl.ds(halo, n), :] = v_ref[...]
    pad_ref[pl.ds(halo + n, halo), :] = jnp.zeros((halo, tc), F32)
    w = w_ref[...]
    bias = b_ref[...]

    def body(c, carry):
        base = pl.multiple_of(c * stride, stride)
        acc = jnp.zeros((stride, tc), F32) + bias
        for k in range(CONV_K):
            acc = acc + w[k:k + 1, :] * pad_ref[pl.ds(base + k * stride, stride), :]
        o_ref[pl.ds(base, stride), :] = acc
        return carry

    lax.fori_loop(0, n // stride, body, 0)


def _conv_col(v, w, b, prev, *, stride, row0, n_col_rows, n_cols, ch0, n_ch, tc=128):
    t, d = v.shape
    r_off, c_off = row0 // n_col_rows, ch0 // tc
    blk = lambda i, j: (i + r_off, j + c_off)
    return pl.pallas_call(
        functools.partial(_conv_col_kernel, stride=stride),
        grid=(n_cols, n_ch // tc),
        in_specs=[
            pl.BlockSpec((n_col_rows, tc), blk),
            pl.BlockSpec((CONV_K, tc), lambda i, j: (0, j + c_off)),
            pl.BlockSpec((1, tc), lambda i, j: (0, j + c_off)),
            pl.BlockSpec(memory_space=pl.ANY),
        ],
        out_specs=pl.BlockSpec((n_col_rows, tc), blk),
        out_shape=jax.ShapeDtypeStruct((t, d), F32),
        scratch_shapes=[pltpu.VMEM((n_col_rows + 2 * CONV_PAD * stride, tc), F32)],
        input_output_aliases={3: 0},
        compiler_params=_cparams("parallel", "parallel"),
    )(v, w, b.reshape(1, d), prev)


def _scan_rows(a, u, carry, reverse):
    n = a.shape[0]
    row = lax.broadcasted_iota(jnp.int32, (SUBLANES, a.shape[1]), 0)
    outs = [None] * (n // SUBLANES)
    order = range(n // SUBLANES - 1, -1, -1) if reverse else range(n // SUBLANES)
    for g in order:
        ag = a[g * SUBLANES:(g + 1) * SUBLANES]
        ug = u[g * SUBLANES:(g + 1) * SUBLANES]
        for s in (1, 2, 4):
            if reverse:
                a_sh = pltpu.roll(ag, SUBLANES - s, 0)
                u_sh = pltpu.roll(ug, SUBLANES - s, 0)
                ok = row < SUBLANES - s
            else:
                a_sh = pltpu.roll(ag, s, 0)
                u_sh = pltpu.roll(ug, s, 0)
                ok = row >= s
            ug = jnp.where(ok, ag * u_sh + ug, ug)
            ag = jnp.where(ok, ag * a_sh, ag)
        hg = ug + ag * carry
        carry = hg[0:1] if reverse else hg[SUBLANES - 1:SUBLANES]
        outs[g] = hg
    return jnp.concatenate(outs, axis=0), carry


def _lru_kernel(p_ref, pprev_ref, pnext_ref, cw_ref, cb_ref, wa_ref, ba_ref, wi_ref, bi_ref,
                lam_ref, *rest, reverse, n_steps, final):
    if final:
        hin_ref, y_ref, o_ref, pad_ref, carry_ref = rest
    else:
        o_ref, pad_ref, carry_ref = rest
    step = pl.program_id(1)
    ts, d = p_ref.shape
    blk = d // LRU_HEADS

    @pl.when(step == 0)
    def _():
        carry_ref[...] = jnp.zeros_like(carry_ref)

    if reverse:
        has_prev = jnp.logical_and(step >= 1, step < n_steps - 1)
        has_next = step >= 2
    else:
        has_prev = step >= 2
        has_next = jnp.logical_and(step >= 1, step < n_steps - 1)
    pad_ref[pl.ds(0, SUBLANES), :] = jnp.where(has_prev, pprev_ref[...], 0.0)
    pad_ref[pl.ds(SUBLANES, ts), :] = p_ref[...]
    pad_ref[pl.ds(SUBLANES + ts, SUBLANES), :] = jnp.where(has_next, pnext_ref[...], 0.0)

    for hd in range(LRU_HEADS):
        cs = slice(hd * blk, (hd + 1) * blk)
        xb = jnp.zeros((ts, blk), F32) + cb_ref[:, cs]
        for k in range(LRU_CONV_K):
            xb = xb + cw_ref[k:k + 1, cs] * pad_ref[pl.ds(SUBLANES - 1 + k, ts), cs]
        xh = xb.astype(BF16)
        r = _sigmoid(_dot(xh, wa_ref[hd]) + ba_ref[:, cs])
        ig = _sigmoid(_dot(xh, wi_ref[hd]) + bi_ref[:, cs])
        z = -lam_ref[:, cs]
        softplus = jnp.maximum(z, 0.0) + jnp.log(1.0 + jnp.exp(-jnp.abs(z)))
        log_a = -LRU_C * r * softplus
        a = jnp.exp(log_a)
        mult = jnp.sqrt(1.0 - jnp.exp(2.0 * log_a))
        u = xb * ig * mult
        hs, carry = _scan_rows(a, u, carry_ref[:, cs], reverse)
        carry_ref[:, cs] = carry
        if final:
            o_ref[:, cs] = (y_ref[:, cs] * (hin_ref[:, cs] + hs)).astype(o_ref.dtype)
        else:
            o_ref[:, cs] = hs


def _lru_direction(p, cw, cb, wa, ba, wi, bi, lam, *, reverse, n_batch, ctx_len, seq, h_in=None, y=None):
    t, d = p.shape
    ts = ROW_TILE
    assert ctx_len == ts
    lat_tiles = seq // ts
    n_steps = lat_tiles + 1
    ctx_tiles = n_batch
    sub_per_tile = ts // SUBLANES
    n_sub = t // SUBLANES
    final = h_in is not None

    def lat_idx(s):
        return (lat_tiles - s) if reverse else (s - 1)

    def tile(b, s):
        return jnp.where(s == 0, n_batch * lat_tiles + b, out_tile(b, s))

    def out_tile(b, s):
        return b * lat_tiles + lat_idx(jnp.maximum(s, 1))

    p_spec = pl.BlockSpec((ts, d), lambda b, s: (tile(b, s), 0))
    prev_spec = pl.BlockSpec((SUBLANES, d), lambda b, s: (jnp.maximum(tile(b, s) * sub_per_tile - 1, 0), 0))
    next_spec = pl.BlockSpec((SUBLANES, d), lambda b, s: (jnp.minimum((tile(b, s) + 1) * sub_per_tile, n_sub - 1), 0))
    vec = pl.BlockSpec((1, d), lambda b, s: (0, 0))
    blk = d // LRU_HEADS
    wspec = pl.BlockSpec((LRU_HEADS, blk, blk), lambda b, s: (0, 0, 0))
    out_spec = pl.BlockSpec((ts, d), lambda b, s: (out_tile(b, s), 0))
    in_specs = [p_spec, prev_spec, next_spec,
                pl.BlockSpec((LRU_CONV_K, d), lambda b, s: (0, 0)), vec,
                wspec, vec, wspec, vec, vec]
    args = [p, p, p, cw, cb.reshape(1, d), wa, ba.reshape(1, d), wi, bi.reshape(1, d), lam.reshape(1, d)]
    if final:
        in_specs += [out_spec, out_spec]
        args += [h_in, y]
    return pl.pallas_call(
        functools.partial(_lru_kernel, reverse=reverse, n_steps=n_steps, final=final),
        grid=(n_batch, n_steps),
        in_specs=in_specs,
        out_specs=out_spec,
        out_shape=jax.ShapeDtypeStruct((n_batch * seq, d), BF16 if final else F32),
        scratch_shapes=[pltpu.VMEM((ts + 2 * SUBLANES, d), F32), pltpu.VMEM((1, d), F32)],
        compiler_params=_cparams("arbitrary", "arbitrary"),
    )(*args)


def _router_kernel(h_ref, g_ref, sh_ref, sc_ref, rw_ref, rb_ref,
                   fn_ref, idx_ref, prob_ref, rank_ref, cnt_ref, carry_ref):
    i = pl.program_id(0)
    tm = h_ref.shape[0]
    n_e = rw_ref.shape[0]

    @pl.when(i == 0)
    def _():
        carry_ref[...] = jnp.zeros_like(carry_ref)

    fn = _rms_mod(h_ref[...], g_ref[...], sh_ref[...], sc_ref[...])
    fn_ref[...] = fn

    f_hi, f_lo = _split_bf16(fn)
    w_hi, w_lo = _split_bf16(rw_ref[...])
    logits = _dot_nt(w_hi, f_hi) + _dot_nt(w_lo, f_hi) + _dot_nt(w_hi, f_lo) + rb_ref[...]

    e_iota = lax.broadcasted_iota(jnp.int32, (n_e, tm), 0).astype(F32)
    taken = jnp.zeros((n_e, tm), F32)
    neg_inf = jnp.float32(-jnp.inf)
    sels, vals, idxs = [], [], []
    for _ in range(TOP_K):
        masked = jnp.where(taken > 0.0, neg_inf, logits)
        m = jnp.max(masked, axis=0, keepdims=True)
        cand = jnp.logical_and(masked == m, taken == 0.0)
        idx = jnp.min(jnp.where(cand, e_iota, float(n_e)), axis=0, keepdims=True)
        sel = e_iota == idx
        taken = jnp.where(sel, 1.0, taken)
        sels.append(sel)
        vals.append(m)
        idxs.append(idx)
    exps = [jnp.exp(v - vals[0]) for v in vals]
    denom = exps[0] + exps[1] + exps[2] + exps[3]
    onehot = jnp.zeros((n_e, tm), F32)
    for sel in sels:
        onehot = onehot + jnp.where(sel, 1.0, 0.0)

    r_iota = lax.broadcasted_iota(jnp.int32, (tm, tm), 0)
    c_iota = lax.broadcasted_iota(jnp.int32, (tm, tm), 1)
    upper = jnp.where(r_iota < c_iota, 1.0, 0.0).astype(BF16)
    before = _dot(onehot.astype(BF16), upper) + carry_ref[:, 0:1]
    for k in range(TOP_K):
        idx_ref[pl.ds(k, 1), :] = idxs[k].astype(jnp.int32)
        prob_ref[pl.ds(k, 1), :] = exps[k] / denom
        rank_ref[pl.ds(k, 1), :] = jnp.sum(jnp.where(sels[k], before, 0.0), axis=0,
                                           keepdims=True).astype(jnp.int32)
    carry_ref[...] = carry_ref[...] + jnp.sum(onehot, axis=1, keepdims=True)
    cnt_ref[...] = carry_ref[...]


def _router(h, g, mods, sh_index, sc_index, rw_t, rb, n_tiles, tile_off):
    t, d = h.shape
    n_e = rw_t.shape[0]
    tm = ROW_TILE
    n_tok = n_tiles * tm
    small = lambda dt: jax.ShapeDtypeStruct((TOP_K, n_tok), dt)
    small_spec = pl.BlockSpec((TOP_K, tm), lambda i: (0, i))
    return pl.pallas_call(
        _router_kernel,
        grid=(n_tiles,),
        in_specs=[
            pl.BlockSpec((tm, d), lambda i: (i + tile_off, 0)),
            pl.BlockSpec((1, d), lambda i: (0, 0)),
            _mod_spec(d, sh_index),
            _mod_spec(d, sc_index),
            pl.BlockSpec((n_e, d), lambda i: (0, 0)),
            pl.BlockSpec((n_e, 1), lambda i: (0, 0)),
        ],
        out_specs=[
            pl.BlockSpec((tm, d), lambda i: (i, 0)),
            small_spec, small_spec, small_spec,
            pl.BlockSpec((n_e, LANES), lambda i: (0, 0)),
        ],
        out_shape=[
            jax.ShapeDtypeStruct((n_tok, d), F32),
            small(jnp.int32), small(F32), small(jnp.int32),
            jax.ShapeDtypeStruct((n_e, LANES), F32),
        ],
        scratch_shapes=[pltpu.VMEM((n_e, LANES), F32)],
        compiler_params=_cparams("arbitrary"),
    )(h, g.reshape(1, d), mods, mods, rw_t, rb.reshape(n_e, 1))


def _dispatch_kernel(vend_ref, gend_ref, pos_ref, x_ref, o_ref, zero_ref, sem, zsem):
    tm = x_ref.shape[0]
    n_e = vend_ref.shape[0]

    @pl.when(pl.program_id(0) == 0)
    def _():
        zero_ref[...] = jnp.zeros_like(zero_ref)

        def tail_copy(j):
            return pltpu.make_async_copy(zero_ref, o_ref.at[pl.ds(pl.multiple_of(j * tm, tm), tm)], zsem)

        def tail_start(j, c):
            tail_copy(j).start()
            return c

        def tail_wait(j, c):
            tail_copy(j).wait()
            return c

        first_tail = gend_ref[n_e - 1] // tm
        lax.fori_loop(first_tail, o_ref.shape[0] // tm, tail_start, 0)
        lax.fori_loop(first_tail, o_ref.shape[0] // tm, tail_wait, 0)

        def clear_copy(r):
            return pltpu.make_async_copy(zero_ref.at[pl.ds(0, 1)], o_ref.at[pl.ds(r, 1)], zsem)

        def clear_start(r, c):
            clear_copy(r).start()
            return c

        def clear_wait(r, c):
            clear_copy(r).wait()
            return c

        def clear(e, c):
            lax.fori_loop(vend_ref[e], gend_ref[e], clear_start, 0)
            lax.fori_loop(vend_ref[e], gend_ref[e], clear_wait, 0)
            return c

        lax.fori_loop(0, vend_ref.shape[0], clear, 0)

    def copy(r, k):
        return pltpu.make_async_copy(x_ref.at[pl.ds(r, 1)], o_ref.at[pl.ds(pos_ref[0, k, r], 1)], sem)

    def start(r, c):
        for k in range(TOP_K):
            copy(r, k).start()
        return c

    def wait(r, c):
        for k in range(TOP_K):
            copy(r, k).wait()
        return c

    lax.fori_loop(0, tm, start, 0)
    lax.fori_loop(0, tm, wait, 0)


def _dispatch(fn, pos_tiles, valid_end, group_end, n_sorted):
    t, d = fn.shape
    tm = ROW_TILE
    grid_spec = pltpu.PrefetchScalarGridSpec(
        num_scalar_prefetch=2,
        grid=(t // tm,),
        in_specs=[
            pl.BlockSpec((1, TOP_K, tm), lambda i, *_: (i, 0, 0), memory_space=pltpu.SMEM),
            pl.BlockSpec((tm, d), lambda i, *_: (i, 0)),
        ],
        out_specs=pl.BlockSpec(memory_space=pl.ANY),
        scratch_shapes=[pltpu.VMEM((tm, d), F32), pltpu.SemaphoreType.DMA(()), pltpu.SemaphoreType.DMA(())],
    )
    return pl.pallas_call(
        _dispatch_kernel,
        grid_spec=grid_spec,
        out_shape=jax.ShapeDtypeStruct((n_sorted, d), F32),
        compiler_params=_cparams("arbitrary"),
    )(valid_end, group_end, pos_tiles, fn)


def _combine_kernel(pos_ref, y_ref, p_ref, h_ref, gate_ref, fg_ref, o_ref, buf_ref, sem, *, final_norm):
    tm = h_ref.shape[0]

    def copy(r, k):
        return pltpu.make_async_copy(y_ref.at[pl.ds(pos_ref[0, k, r], 1)], buf_ref.at[k, pl.ds(r, 1)], sem)

    def start(r, c):
        for k in range(TOP_K):
            copy(r, k).start()
        return c

    def wait(r, c):
        for k in range(TOP_K):
            copy(r, k).wait()
        return c

    lax.fori_loop(0, tm, start, 0)
    lax.fori_loop(0, tm, wait, 0)
    p = p_ref[...]
    acc = p[:, 0:1] * buf_ref[0]
    for k in range(1, TOP_K):
        acc = acc + p[:, k:k + 1] * buf_ref[k]
    out = h_ref[...] + gate_ref[...] * acc
    if final_norm:
        ms = jnp.mean(out * out, axis=-1, keepdims=True)
        out = out * lax.rsqrt(ms + EPS) * fg_ref[...]
    o_ref[...] = out


def _combine(y, pos_tiles, probs, h, mods, gate_index, final_g, final_norm, tm=128):
    n_tok = probs.shape[0]
    d = y.shape[1]
    return pl.pallas_call(
        functools.partial(_combine_kernel, final_norm=final_norm),
        grid=(n_tok // tm,),
        in_specs=[
            pl.BlockSpec((1, TOP_K, tm), lambda i: (i, 0, 0), memory_space=pltpu.SMEM),
            pl.BlockSpec(memory_space=pl.ANY),
            pl.BlockSpec((tm, TOP_K), lambda i: (i, 0)),
            pl.BlockSpec((tm, d), lambda i: (i, 0)),
            _mod_spec(d, gate_index),
            pl.BlockSpec((1, d), lambda i: (0, 0)),
        ],
        out_specs=pl.BlockSpec((tm, d), lambda i: (i, 0)),
        out_shape=jax.ShapeDtypeStruct((n_tok, d), F32),
        scratch_shapes=[pltpu.VMEM((TOP_K, tm, d), F32), pltpu.SemaphoreType.DMA(())],
        compiler_params=_cparams("arbitrary"),
    )(pos_tiles, y, probs, h, mods, final_g.reshape(1, d))


W_CHUNK = 512
W_RING = 3


def _experts_kernel(te_ref, tv_ref, xt_ref, slot_ref, nxt_ref, c0_ref, c1_ref,
                    x_ref, b1_ref, b2_ref, w1_hbm, w2_hbm, o_ref,
                    wbf_ref, ring_ref, act_ref, sems, *, layer, n1, n_chunks):
    del xt_ref
    i = pl.program_id(0)
    e_cur = te_ref[i]
    slot = slot_ref[i]
    nxt = nxt_ref[i]
    d = x_ref.shape[1]
    f = act_ref.shape[1]

    def start_chunk(e, c):
        r = lax.rem(c, W_RING)

        @pl.when(c < n1)
        def _():
            pltpu.make_async_copy(w1_hbm.at[layer, e, pl.ds(c * W_CHUNK, W_CHUNK)],
                                  ring_ref.at[r], sems.at[r]).start()

        @pl.when(c >= n1)
        def _():
            pltpu.make_async_copy(w2_hbm.at[layer, e, pl.ds((c - n1) * W_CHUNK, W_CHUNK)],
                                  ring_ref.at[r], sems.at[r]).start()

    def land_chunks(e, dst_slot, c_lo, c_hi):
        def body(c, carry):
            r = lax.rem(c, W_RING)
            pltpu.make_async_copy(w1_hbm.at[layer, 0, pl.ds(0, W_CHUNK)], ring_ref.at[r], sems.at[r]).wait()
            wbf_ref[dst_slot, pl.ds(pl.multiple_of(c * W_CHUNK, W_CHUNK), W_CHUNK), :] = ring_ref[r].astype(BF16)

            @pl.when(c + W_RING < n_chunks)
            def _():
                start_chunk(e, c + W_RING)
            return carry

        lax.fori_loop(c_lo, c_hi, body, 0)

    @pl.when(i == 0)
    def _():
        for c in range(W_RING):
            start_chunk(e_cur, jnp.int32(c))
        land_chunks(e_cur, slot, 0, n_chunks)

    first = jnp.logical_or(i == 0, e_cur != te_ref[jnp.maximum(i - 1, 0)])

    @pl.when(jnp.logical_and(first, nxt >= 0))
    def _():
        for c in range(W_RING):
            start_chunk(nxt, jnp.int32(c))

    @pl.when(tv_ref[i] > 0)
    def _():
        x = x_ref[...].astype(BF16)
        tn = 512
        for j in range(f // tn):
            ca = slice(j * tn, (j + 1) * tn)
            cb = slice(f + j * tn, f + (j + 1) * tn)
            glu = _dot(x, wbf_ref[slot, 0:d, ca]) + b1_ref[0, :, ca]
            lin = _dot(x, wbf_ref[slot, 0:d, cb]) + b1_ref[0, :, cb]
            glu = jnp.minimum(glu, SWIGLU_LIMIT)
            lin = jnp.clip(lin, -SWIGLU_LIMIT, SWIGLU_LIMIT)
            act_ref[:, ca] = (glu * _sigmoid(SWIGLU_ALPHA * glu) * (lin + 1.0)).astype(BF16)
        o_ref[...] = _dot(act_ref[...], wbf_ref[slot, d:d + f, :]) + b2_ref[0]

    @pl.when(tv_ref[i] == 0)
    def _():
        o_ref[...] = jnp.zeros_like(o_ref)

    @pl.when(nxt >= 0)
    def _():
        land_chunks(nxt, 1 - slot, c0_ref[i], c1_ref[i])


def _experts(xs, sched, w1, b1, w2, b2, layer):
    d = xs.shape[1]
    n_tiles = sched[0].shape[0]
    _, n_e, _, two_f = w1.shape
    f = two_f // 2
    assert d == two_f and d % W_CHUNK == 0 and f % W_CHUNK == 0
    n1, n_chunks = d // W_CHUNK, (d + f) // W_CHUNK
    tm = ROW_TILE
    grid_spec = pltpu.PrefetchScalarGridSpec(
        num_scalar_prefetch=7,
        grid=(n_tiles,),
        in_specs=[
            pl.BlockSpec((tm, d), lambda i, te, tv, xt, *_: (xt[i], 0)),
            pl.BlockSpec((1, 1, two_f), lambda i, te, *_: (layer * n_e + te[i], 0, 0)),
            pl.BlockSpec((1, 1, d), lambda i, te, *_: (layer * n_e + te[i], 0, 0)),
            pl.BlockSpec(memory_space=pl.ANY),
            pl.BlockSpec(memory_space=pl.ANY),
        ],
        out_specs=pl.BlockSpec((tm, d), lambda i, *_: (i, 0)),
        scratch_shapes=[
            pltpu.VMEM((2, d + f, d), BF16),
            pltpu.VMEM((W_RING, W_CHUNK, d), F32),
            pltpu.VMEM((tm, f), BF16),
            pltpu.SemaphoreType.DMA((W_RING,)),
        ],
    )
    return pl.pallas_call(
        functools.partial(_experts_kernel, layer=layer, n1=n1, n_chunks=n_chunks),
        grid_spec=grid_spec,
        out_shape=jax.ShapeDtypeStruct((n_tiles * tm, d), F32),
        compiler_params=_cparams("arbitrary"),
    )(*sched, xs, b1.reshape(-1, 1, two_f), b2.reshape(-1, 1, d), w1, w2)


def _moe(h, g, mods, mod_index, router_w, router_b, w1, b1, w2, b2, layer, final_g, n_tiles, final_norm):
    tm = ROW_TILE
    n_e = router_w.shape[1]
    n_tok = n_tiles * tm
    fn, idx, prob, rank, cnt = _router(h, g, mods, mod_index(3, tm), mod_index(4, tm), router_w.T, router_b,
                                       n_tiles, 0)

    i32 = jnp.int32
    counts = cnt[:, 0].astype(i32)
    e_tiles = (counts + tm - 1) // tm
    e_end = jnp.cumsum(e_tiles)
    e_start = e_end - e_tiles
    experts = jnp.arange(n_e, dtype=i32)
    onehot = (idx[None] == experts[:, None, None]).astype(F32)
    group_start = (e_start * tm).astype(F32)
    pos = jnp.einsum("ekt,e->kt", onehot, group_start, precision=lax.Precision.HIGHEST).astype(i32) + rank
    n_tiles_sorted = (n_tok * TOP_K) // tm + n_e
    tiles = jnp.arange(n_tiles_sorted, dtype=i32)
    tile_valid = (tiles < e_end[-1]).astype(i32)
    last_tile = jnp.maximum(e_end[-1] - 1, 0)
    tile_expert = jnp.sum((e_end[None, :] <= jnp.minimum(tiles, last_tile)[:, None]).astype(i32), axis=1)
    tile_expert = jnp.minimum(tile_expert, n_e - 1)
    nonempty = e_tiles > 0
    later = jnp.where(nonempty[None, :] & (experts[None, :] > experts[:, None]), experts[None, :], n_e)
    e_next = jnp.min(later, axis=1)
    e_next = jnp.where(e_next < n_e, e_next, -1)
    e_slot = (jnp.cumsum(nonempty.astype(i32)) - nonempty.astype(i32)) % 2
    n_chunks = (w1.shape[2] + w2.shape[2]) // W_CHUNK
    local = tiles - e_start[tile_expert]
    per = jnp.maximum(e_tiles[tile_expert], 1)
    tile_next = jnp.where(tile_valid > 0, e_next[tile_expert], -1)
    has_next = tile_next >= 0
    c_lo = jnp.where(has_next, (n_chunks * local) // per, 0)
    c_hi = jnp.where(has_next, (n_chunks * (local + 1)) // per, 0)
    x_tile = jnp.minimum(tiles, last_tile)
    sched = (tile_expert, tile_valid, x_tile, e_slot[tile_expert], tile_next, c_lo, c_hi)

    def per_tile(a, rows):
        return a.reshape(TOP_K, n_tok // rows, rows).transpose(1, 0, 2)

    xs = _dispatch(fn, per_tile(pos, tm), e_start * tm + counts, e_end * tm, n_tiles_sorted * tm)
    y = _experts(xs, sched, w1, b1, w2, b2, layer)
    tmc = 128
    return _combine(y, per_tile(pos, tmc), prob.T, h, mods, mod_index(5, tmc), final_g, final_norm, tm=tmc)


def kernel(x, c, ctx, c_ctx, ada_w, ada_b, norm1_g, norm2_g, conv_pw1_w, conv_pw1_b, conv_dw_w, conv_dw_b, conv_ln_g, conv_ln_b, conv_pw2_w, conv_pw2_b, lru_wy, lru_by, lru_wx, lru_bx, lru_conv_w, lru_conv_b, lru_wa, lru_ba, lru_wi, lru_bi, lru_lambda, lru_wo, lru_bo, router_w, router_b, moe_w1, moe_b1, moe_w2, moe_b2, final_g):
    n_batch, seq, d = x.shape
    ctx_len = ctx.shape[1]
    depth = ada_w.shape[0]
    tm = ROW_TILE
    n_ctx, n_lat = n_batch * ctx_len, n_batch * seq
    t_all = n_ctx + n_lat
    lat_tiles, all_tiles = n_lat // tm, t_all // tm
    ctx_src = n_batch
    n_src = n_batch + 1
    big = 512
    assert depth == 2 and ctx_len == tm and seq % big == 0 and n_ctx % big == 0 and n_src <= SUBLANES

    src = jnp.zeros((SUBLANES, d), F32).at[:n_batch].set(c).at[ctx_src].set(c_ctx)
    mods = _ada(src, ada_w, ada_b)[:, :n_src].reshape(depth * n_src * N_ADA, 1, d)

    def mod_index(layer):
        def for_chunk(chunk, rows):
            def index(i):
                row = i * rows
                tile_src = jnp.where(row >= n_lat, ctx_src, row // seq)
                return (layer * n_src + tile_src) * N_ADA + chunk
            return index
        return for_chunk

    bf = lambda a: a.astype(BF16)
    h = jnp.concatenate([x.reshape(n_lat, d), ctx.reshape(n_ctx, d)], axis=0)

    mi = mod_index(0)
    xn = _norm_mod(h, norm1_g[0], mods, mi(0, tm), mi(1, tm), all_tiles)
    v = _mm_glu(xn, bf(conv_pw1_w[0]), conv_pw1_b[0], tm=tm, tn=d)
    half = d // 2
    dw_w, dw_b = conv_dw_w[0], conv_dw_b[0]
    cv = jnp.zeros((t_all, d), F32)
    cv = _conv_seg(v, dw_w, dw_b, cv, seg=ctx_len, row0=n_lat, n_rows=n_ctx, ch0=0, n_ch=d, rows_blk=ctx_len)
    cv = _conv_seg(v, dw_w, dw_b, cv, seg=GRID_W, row0=0, n_rows=n_lat, ch0=0, n_ch=half)
    cv = _conv_col(v, dw_w, dw_b, cv, stride=GRID_W, row0=0, n_col_rows=seq, n_cols=n_batch, ch0=half, n_ch=half)
    h = _mm_ln_res(cv, conv_ln_g[0], conv_ln_b[0], bf(conv_pw2_w[0]), conv_pw2_b[0], h, mods, mi(2, tm),
                   tm=tm, tn=d)
    h = _moe(h, norm2_g[0], mods, mi, router_w[0], router_b[0], moe_w1, moe_b1, moe_w2, moe_b2, 0,
             final_g, all_tiles, False)

    mi = mod_index(1)
    xn = _norm_mod(h, norm1_g[1], mods, mi(0, tm), mi(1, tm), all_tiles)
    p, yl = _mm_pair(xn, bf(lru_wx[0]), lru_bx[0], bf(lru_wy[0]), lru_by[0], tm=tm, tn=d)
    common = dict(n_batch=n_batch, ctx_len=ctx_len, seq=seq)
    hf = _lru_direction(p, lru_conv_w[0], lru_conv_b[0], bf(lru_wa[0, 0]), lru_ba[0, 0], bf(lru_wi[0, 0]),
                        lru_bi[0, 0], lru_lambda[0, 0], reverse=False, **common)
    z = _lru_direction(p, lru_conv_w[0], lru_conv_b[0], bf(lru_wa[0, 1]), lru_ba[0, 1], bf(lru_wi[0, 1]),
                       lru_bi[0, 1], lru_lambda[0, 1], reverse=True, h_in=hf, y=yl, **common)
    h_lat = _mm_res(z, bf(lru_wo[0]), lru_bo[0], h, mods, mi(2, big), 0, tm=big, tn=d)
    out = _moe(h_lat, norm2_g[1], mods, mi, router_w[1], router_b[1], moe_w1, moe_b1, moe_w2, moe_b2, 1,
               final_g, lat_tiles, True)
    return out.reshape(n_batch, seq, d)
```

```python
import functools

import jax
import jax.numpy as jnp
from jax import lax
from jax.experimental import pallas as pl
from jax.experimental.pallas import tpu as pltpu

F32 = jnp.float32
BF16 = jnp.bfloat16

GRID_W = 64
N_ADA = 6
EPS = 1e-6
CONV_K = 31
CONV_PAD = CONV_K // 2
LRU_HEADS = 16
LRU_C = 8.0
LRU_CONV_K = 4
TOP_K = 4
SWIGLU_ALPHA = 1.702
SWIGLU_LIMIT = 7.0

ROW_TILE = 256
SUBLANES = 8
LANES = 128
VMEM_LIMIT = 56 * 1024 * 1024


def _cparams(*sem):
    return pltpu.CompilerParams(dimension_semantics=sem, vmem_limit_bytes=VMEM_LIMIT)


def _sigmoid(x):
    return 1.0 / (1.0 + jnp.exp(-x))


def _split_bf16(a):
    hi = a.astype(BF16)
    lo = (a - hi.astype(F32)).astype(BF16)
    return hi, lo


def _dot(a, b):
    return jnp.dot(a, b, preferred_element_type=F32)


def _dot_nt(a, b):
    return lax.dot_general(a, b, (((1,), (1,)), ((), ())), preferred_element_type=F32)


def _ada_kernel(s_ref, w_ref, b_ref, o_ref):
    s = s_ref[...]
    s = s * _sigmoid(s)
    s_hi, s_lo = _split_bf16(s)
    w_hi, w_lo = _split_bf16(w_ref[0])
    acc = _dot(s_hi, w_hi) + _dot(s_hi, w_lo) + _dot(s_lo, w_hi)
    o_ref[0] = acc + b_ref[0]


def _ada(src, ada_w, ada_b, tn=1024):
    depth, d, n = ada_w.shape
    return pl.pallas_call(
        _ada_kernel,
        grid=(depth, n // tn),
        in_specs=[
            pl.BlockSpec((SUBLANES, d), lambda l, j: (0, 0)),
            pl.BlockSpec((1, d, tn), lambda l, j: (l, 0, j)),
            pl.BlockSpec((1, 1, tn), lambda l, j: (l, 0, j)),
        ],
        out_specs=pl.BlockSpec((1, SUBLANES, tn), lambda l, j: (l, 0, j)),
        out_shape=jax.ShapeDtypeStruct((depth, SUBLANES, n), F32),
        compiler_params=_cparams("parallel", "parallel"),
    )(src, ada_w, ada_b.reshape(depth, 1, n))


def _rms_mod(x, g, sh, sc):
    ms = jnp.mean(x * x, axis=-1, keepdims=True)
    y = x * lax.rsqrt(ms + EPS) * g
    return y * (1.0 + sc) + sh


def _norm_mod_kernel(h_ref, g_ref, sh_ref, sc_ref, o_ref):
    o_ref[...] = _rms_mod(h_ref[...], g_ref[...], sh_ref[...], sc_ref[...]).astype(o_ref.dtype)


def _mod_spec(d, mod_index):
    return pl.BlockSpec((None, 1, d), lambda i, *_: (mod_index(i), 0, 0))


def _norm_mod(h, g, mods, sh_index, sc_index, n_tiles, tile_off=0):
    t, d = h.shape
    return pl.pallas_call(
        _norm_mod_kernel,
        grid=(n_tiles,),
        in_specs=[
            pl.BlockSpec((ROW_TILE, d), lambda i: (i + tile_off, 0)),
            pl.BlockSpec((1, d), lambda i: (0, 0)),
            _mod_spec(d, sh_index),
            _mod_spec(d, sc_index),
        ],
        out_specs=pl.BlockSpec((ROW_TILE, d), lambda i: (i, 0)),
        out_shape=jax.ShapeDtypeStruct((n_tiles * ROW_TILE, d), BF16),
        compiler_params=_cparams("parallel"),
    )(h, g.reshape(1, d), mods, mods)


def _two_source_specs(n_head, n_tail, block):
    head = pl.BlockSpec(block, lambda i, *_: (jnp.minimum(i, n_head - 1), 0))
    tail = pl.BlockSpec(block, lambda i, *_: (jnp.clip(i - n_head, 0, n_tail - 1), 0))
    return head, tail


def _norm_mod2_kernel(a_ref, b_ref, g_ref, sh_ref, sc_ref, o_ref, *, n_head):
    x = jnp.where(pl.program_id(0) < n_head, a_ref[...], b_ref[...])
    o_ref[...] = _rms_mod(x, g_ref[...], sh_ref[...], sc_ref[...]).astype(o_ref.dtype)


def _norm_mod2(a, b, g, mods, sh_index, sc_index):
    d = a.shape[1]
    n_head, n_tail = a.shape[0] // ROW_TILE, b.shape[0] // ROW_TILE
    head, tail = _two_source_specs(n_head, n_tail, (ROW_TILE, d))
    return pl.pallas_call(
        functools.partial(_norm_mod2_kernel, n_head=n_head),
        grid=(n_head + n_tail,),
        in_specs=[head, tail, pl.BlockSpec((1, d), lambda i: (0, 0)), _mod_spec(d, sh_index), _mod_spec(d, sc_index)],
        out_specs=pl.BlockSpec((ROW_TILE, d), lambda i: (i, 0)),
        out_shape=jax.ShapeDtypeStruct(((n_head + n_tail) * ROW_TILE, d), BF16),
        compiler_params=_cparams("parallel"),
    )(a, b, g.reshape(1, d), mods, mods)


def _layernorm_silu(x, g, b):
    mu = jnp.mean(x, axis=-1, keepdims=True)
    xc = x - mu
    y = xc * lax.rsqrt(jnp.mean(xc * xc, axis=-1, keepdims=True) + EPS)
    y = y * g + b
    return y * _sigmoid(y)


def _gelu_tanh(x):
    return 0.5 * x * (1.0 + jnp.tanh(0.7978845608028654 * (x + 0.044715 * (x * x * x))))


def _mm_pair_kernel(x_ref, wa_ref, wb_ref, ba_ref, bb_ref, oa_ref, ob_ref):
    x = x_ref[...]
    oa_ref[...] = _dot(x, wa_ref[...]) + ba_ref[...]
    ob_ref[...] = _gelu_tanh(_dot(x, wb_ref[...]) + bb_ref[...])


def _mm_glu_kernel(x_ref, wa_ref, wb_ref, ba_ref, bb_ref, o_ref):
    x = x_ref[...]
    ua = _dot(x, wa_ref[...]) + ba_ref[...]
    ub = _dot(x, wb_ref[...]) + bb_ref[...]
    o_ref[...] = ua * _sigmoid(ub)


def _mm_ln_res_kernel(x_ref, lg_ref, lb_ref, w_ref, b_ref, ha_ref, hb_ref, gate_ref, o_ref, xs_ref, *, n_head):
    @pl.when(pl.program_id(1) == 0)
    def _():
        xs_ref[...] = _layernorm_silu(x_ref[...], lg_ref[...], lb_ref[...]).astype(BF16)

    acc = _dot(xs_ref[...], w_ref[...]) + b_ref[...]
    h = jnp.where(pl.program_id(0) < n_head, ha_ref[...], hb_ref[...])
    o_ref[...] = h + gate_ref[...] * acc


def _mm_res_kernel(x_ref, w_ref, b_ref, h_ref, gate_ref, o_ref):
    acc = _dot(x_ref[...], w_ref[...]) + b_ref[...]
    o_ref[...] = h_ref[...] + gate_ref[...] * acc


def _mm_pair(x, wa, ba, wb, bb, tm=512, tn=1024):
    t, k = x.shape
    n = wa.shape[1]
    wspec = pl.BlockSpec((k, tn), lambda i, j: (0, j))
    bspec = pl.BlockSpec((1, tn), lambda i, j: (0, j))
    ospec = pl.BlockSpec((tm, tn), lambda i, j: (i, j))
    return pl.pallas_call(
        _mm_pair_kernel,
        grid=(t // tm, n // tn),
        in_specs=[pl.BlockSpec((tm, k), lambda i, j: (i, 0)), wspec, wspec, bspec, bspec],
        out_specs=[ospec, ospec],
        out_shape=[jax.ShapeDtypeStruct((t, n), F32)] * 2,
        compiler_params=_cparams("parallel", "arbitrary"),
    )(x, wa, wb, ba.reshape(1, n), bb.reshape(1, n))


def _mm_glu(x, w, b, tm=512, tn=1024):
    t, k = x.shape
    n = w.shape[1] // 2
    nj = n // tn
    b2 = b.reshape(1, 2 * n)
    return pl.pallas_call(
        _mm_glu_kernel,
        grid=(t // tm, nj),
        in_specs=[
            pl.BlockSpec((tm, k), lambda i, j: (i, 0)),
            pl.BlockSpec((k, tn), lambda i, j: (0, j)),
            pl.BlockSpec((k, tn), lambda i, j: (0, j + nj)),
            pl.BlockSpec((1, tn), lambda i, j: (0, j)),
            pl.BlockSpec((1, tn), lambda i, j: (0, j + nj)),
        ],
        out_specs=pl.BlockSpec((tm, tn), lambda i, j: (i, j)),
        out_shape=jax.ShapeDtypeStruct((t, n), F32),
        compiler_params=_cparams("parallel", "arbitrary"),
    )(x, w, w, b2, b2)


def _mm_ln_res(x, ln_g, ln_b, w, b, h_head, h_tail, mods, gate_index, tm=ROW_TILE):
    t, k = x.shape
    n = w.shape[1]
    n_head, n_tail = h_head.shape[0] // tm, h_tail.shape[0] // tm
    head, tail = _two_source_specs(n_head, n_tail, (tm, n))
    return pl.pallas_call(
        functools.partial(_mm_ln_res_kernel, n_head=n_head),
        grid=(t // tm, 1),
        in_specs=[
            pl.BlockSpec((tm, k), lambda i, j: (i, 0)),
            pl.BlockSpec((1, k), lambda i, j: (0, 0)),
            pl.BlockSpec((1, k), lambda i, j: (0, 0)),
            pl.BlockSpec((k, n), lambda i, j: (0, 0)),
            pl.BlockSpec((1, n), lambda i, j: (0, 0)),
            head,
            tail,
            pl.BlockSpec((None, 1, n), lambda i, j: (gate_index(i), 0, 0)),
        ],
        out_specs=pl.BlockSpec((tm, n), lambda i, j: (i, 0)),
        out_shape=jax.ShapeDtypeStruct((t, n), F32),
        scratch_shapes=[pltpu.VMEM((tm, k), BF16)],
        compiler_params=_cparams("parallel", "arbitrary"),
    )(x, ln_g.reshape(1, k), ln_b.reshape(1, k), w, b.reshape(1, n), h_head, h_tail, mods)


def _mm_res(x, w, b, h, mods, gate_index, h_tile_off, tm=512, tn=1024):
    t, k = x.shape
    n = w.shape[1]
    return pl.pallas_call(
        _mm_res_kernel,
        grid=(t // tm, n // tn),
        in_specs=[
            pl.BlockSpec((tm, k), lambda i, j: (i, 0)),
            pl.BlockSpec((k, tn), lambda i, j: (0, j)),
            pl.BlockSpec((1, tn), lambda i, j: (0, j)),
            pl.BlockSpec((tm, tn), lambda i, j: (i + h_tile_off, j)),
            pl.BlockSpec((None, 1, tn), lambda i, j: (gate_index(i), 0, j)),
        ],
        out_specs=pl.BlockSpec((tm, tn), lambda i, j: (i, j)),
        out_shape=jax.ShapeDtypeStruct((t, n), F32),
        compiler_params=_cparams("parallel", "arbitrary"),
    )(x, w, b.reshape(1, n), h, mods)


SEG_PAD = 16


def _conv_seg_kernel(v_ref, w_ref, b_ref, prev_ref, o_ref, pad_ref, *, seg, n_seg):
    del prev_ref
    tc = v_ref.shape[1]
    zeros = jnp.zeros((SEG_PAD, tc), F32)
    for s in range(n_seg):
        pad_ref[s, pl.ds(0, SEG_PAD), :] = zeros
        pad_ref[s, pl.ds(SEG_PAD, seg), :] = v_ref[pl.ds(s * seg, seg), :]
        pad_ref[s, pl.ds(SEG_PAD + seg, SEG_PAD), :] = zeros
    w = w_ref[...]
    bias = b_ref[...]
    chunk = min(seg, 64)
    for s in range(n_seg):
        for c0 in range(0, seg, chunk):
            acc = jnp.zeros((chunk, tc), F32) + bias
            for k in range(CONV_K):
                off = SEG_PAD - CONV_PAD + k + c0
                acc = acc + w[k:k + 1, :] * pad_ref[s, pl.ds(off, chunk), :]
            o_ref[pl.ds(s * seg + c0, chunk), :] = acc


def _conv_seg(v, w, b, prev, *, seg, row0, n_rows, ch0, n_ch, rows_blk=512, tc=128):
    t, d = v.shape
    n_seg = rows_blk // seg
    r_off, c_off = row0 // rows_blk, ch0 // tc
    blk = lambda i, j: (i + r_off, j + c_off)
    return pl.pallas_call(
        functools.partial(_conv_seg_kernel, seg=seg, n_seg=n_seg),
        grid=(n_rows // rows_blk, n_ch // tc),
        in_specs=[
            pl.BlockSpec((rows_blk, tc), blk),
            pl.BlockSpec((CONV_K, tc), lambda i, j: (0, j + c_off)),
            pl.BlockSpec((1, tc), lambda i, j: (0, j + c_off)),
            pl.BlockSpec(memory_space=pl.ANY),
        ],
        out_specs=pl.BlockSpec((rows_blk, tc), blk),
        out_shape=jax.ShapeDtypeStruct((t, d), F32),
        scratch_shapes=[pltpu.VMEM((n_seg, seg + 2 * SEG_PAD, tc), F32)],
        input_output_aliases={3: 0},
        compiler_params=_cparams("parallel", "parallel"),
    )(v, w, b.reshape(1, d), prev)


def _conv_col_kernel(v_ref, w_ref, b_ref, prev_ref, o_ref, pad_ref, *, stride):
    del prev_ref
    n, tc = v_ref.shape
    halo = CONV_PAD * stride
    pad_ref[pl.ds(0, halo), :] = jnp.zeros((halo, tc), F32)
    pad_ref[pl.ds(halo, n), :] = v_ref[...]
    pad_ref[pl.ds(halo + n, halo), :] = jnp.zeros((halo, tc), F32)
    w = w_ref[...]
    bias = b_ref[...]

    def body(c, carry):
        base = pl.multiple_of(c * stride, stride)
        acc = jnp.zeros((stride, tc), F32) + bias
        for k in range(CONV_K):
            acc = acc + w[k:k + 1, :] * pad_ref[pl.ds(base + k * stride, stride), :]
        o_ref[pl.ds(base, stride), :] = acc
        return carry

    lax.fori_loop(0, n // stride, body, 0)


def _conv_col(v, w, b, prev, *, stride, row0, n_col_rows, n_cols, ch0, n_ch, tc=128):
    t, d = v.shape
    r_off, c_off = row0 // n_col_rows, ch0 // tc
    blk = lambda i, j: (i + r_off, j + c_off)
    return pl.pallas_call(
        functools.partial(_conv_col_kernel, stride=stride),
        grid=(n_cols, n_ch // tc),
        in_specs=[
            pl.BlockSpec((n_col_rows, tc), blk),
            pl.BlockSpec((CONV_K, tc), lambda i, j: (0, j + c_off)),
            pl.BlockSpec((1, tc), lambda i, j: (0, j + c_off)),
            pl.BlockSpec(memory_space=pl.ANY),
        ],
        out_specs=pl.BlockSpec((n_col_rows, tc), blk),
        out_shape=jax.ShapeDtypeStruct((t, d), F32),
        scratch_shapes=[pltpu.VMEM((n_col_rows + 2 * CONV_PAD * stride, tc), F32)],
        input_output_aliases={3: 0},
        compiler_params=_cparams("parallel", "parallel"),
    )(v, w, b.reshape(1, d), prev)


def _scan_rows(a, u, carry, reverse):
    n = a.shape[0]
    row = lax.broadcasted_iota(jnp.int32, (SUBLANES, a.shape[1]), 0)
    outs = [None] * (n // SUBLANES)
    order = range(n // SUBLANES - 1, -1, -1) if reverse else range(n // SUBLANES)
    for g in order:
        ag = a[g * SUBLANES:(g + 1) * SUBLANES]
        ug = u[g * SUBLANES:(g + 1) * SUBLANES]
        for s in (1, 2, 4):
            if reverse:
                a_sh = pltpu.roll(ag, SUBLANES - s, 0)
                u_sh = pltpu.roll(ug, SUBLANES - s, 0)
                ok = row < SUBLANES - s
            else:
                a_sh = pltpu.roll(ag, s, 0)
                u_sh = pltpu.roll(ug, s, 0)
                ok = row >= s
            ug = jnp.where(ok, ag * u_sh + ug, ug)
            ag = jnp.where(ok, ag * a_sh, ag)
        hg = ug + ag * carry
        carry = hg[0:1] if reverse else hg[SUBLANES - 1:SUBLANES]
        outs[g] = hg
    return jnp.concatenate(outs, axis=0), carry


def _lru_kernel(p_ref, pprev_ref, pnext_ref, cw_ref, cb_ref, wa_ref, ba_ref, wi_ref, bi_ref,
                lam_ref, *rest, reverse, n_steps, final):
    if final:
        hin_ref, y_ref, o_ref, pad_ref, carry_ref = rest
    else:
        o_ref, pad_ref, carry_ref = rest
    step = pl.program_id(1)
    ts, d = p_ref.shape
    blk = d // LRU_HEADS

    @pl.when(step == 0)
    def _():
        carry_ref[...] = jnp.zeros_like(carry_ref)

    if reverse:
        has_prev = jnp.logical_and(step >= 1, step < n_steps - 1)
        has_next = step >= 2
    else:
        has_prev = step >= 2
        has_next = jnp.logical_and(step >= 1, step < n_steps - 1)
    pad_ref[pl.ds(0, SUBLANES), :] = jnp.where(has_prev, pprev_ref[...], 0.0)
    pad_ref[pl.ds(SUBLANES, ts), :] = p_ref[...]
    pad_ref[pl.ds(SUBLANES + ts, SUBLANES), :] = jnp.where(has_next, pnext_ref[...], 0.0)

    for hd in range(LRU_HEADS):
        cs = slice(hd * blk, (hd + 1) * blk)
        xb = jnp.zeros((ts, blk), F32) + cb_ref[:, cs]
        for k in range(LRU_CONV_K):
            xb = xb + cw_ref[k:k + 1, cs] * pad_ref[pl.ds(SUBLANES - 1 + k, ts), cs]
        xh = xb.astype(BF16)
        r = _sigmoid(_dot(xh, wa_ref[hd]) + ba_ref[:, cs])
        ig = _sigmoid(_dot(xh, wi_ref[hd]) + bi_ref[:, cs])
        z = -lam_ref[:, cs]
        softplus = jnp.maximum(z, 0.0) + jnp.log(1.0 + jnp.exp(-jnp.abs(z)))
        log_a = -LRU_C * r * softplus
        a = jnp.exp(log_a)
        mult = jnp.sqrt(1.0 - a * a)
        u = xb * ig * mult
        hs, carry = _scan_rows(a, u, carry_ref[:, cs], reverse)
        carry_ref[:, cs] = carry
        if final:
            o_ref[:, cs] = (y_ref[:, cs] * (hin_ref[:, cs] + hs)).astype(o_ref.dtype)
        else:
            o_ref[:, cs] = hs


def _lru_direction(p, cw, cb, wa, ba, wi, bi, lam, *, reverse, n_batch, ctx_len, seq, h_in=None, y=None):
    t, d = p.shape
    ts = ROW_TILE
    assert ctx_len == ts
    lat_tiles = seq // ts
    n_steps = lat_tiles + 1
    ctx_tiles = n_batch
    sub_per_tile = ts // SUBLANES
    n_sub = t // SUBLANES
    final = h_in is not None

    def lat_idx(s):
        return (lat_tiles - s) if reverse else (s - 1)

    def tile(b, s):
        return jnp.where(s == 0, n_batch * lat_tiles + b, out_tile(b, s))

    def out_tile(b, s):
        return b * lat_tiles + lat_idx(jnp.maximum(s, 1))

    p_spec = pl.BlockSpec((ts, d), lambda b, s: (tile(b, s), 0))
    prev_spec = pl.BlockSpec((SUBLANES, d), lambda b, s: (jnp.maximum(tile(b, s) * sub_per_tile - 1, 0), 0))
    next_spec = pl.BlockSpec((SUBLANES, d), lambda b, s: (jnp.minimum((tile(b, s) + 1) * sub_per_tile, n_sub - 1), 0))
    vec = pl.BlockSpec((1, d), lambda b, s: (0, 0))
    blk = d // LRU_HEADS
    wspec = pl.BlockSpec((LRU_HEADS, blk, blk), lambda b, s: (0, 0, 0))
    out_spec = pl.BlockSpec((ts, d), lambda b, s: (out_tile(b, s), 0))
    in_specs = [p_spec, prev_spec, next_spec,
                pl.BlockSpec((LRU_CONV_K, d), lambda b, s: (0, 0)), vec,
                wspec, vec, wspec, vec, vec]
    args = [p, p, p, cw, cb.reshape(1, d), wa, ba.reshape(1, d), wi, bi.reshape(1, d), lam.reshape(1, d)]
    if final:
        in_specs += [out_spec, out_spec]
        args += [h_in, y]
    return pl.pallas_call(
        functools.partial(_lru_kernel, reverse=reverse, n_steps=n_steps, final=final),
        grid=(n_batch, n_steps),
        in_specs=in_specs,
        out_specs=out_spec,
        out_shape=jax.ShapeDtypeStruct((n_batch * seq, d), BF16 if final else F32),
        scratch_shapes=[pltpu.VMEM((ts + 2 * SUBLANES, d), F32), pltpu.VMEM((1, d), F32)],
        compiler_params=_cparams("arbitrary", "arbitrary"),
    )(*args)


def _router_kernel(h_ref, g_ref, sh_ref, sc_ref, rw_ref, rb_ref,
                   fn_ref, idx_ref, prob_ref, rank_ref, cnt_ref, carry_ref):
    i = pl.program_id(0)
    tm = h_ref.shape[0]
    n_e = rw_ref.shape[0]

    @pl.when(i == 0)
    def _():
        carry_ref[...] = jnp.zeros_like(carry_ref)

    fn = _rms_mod(h_ref[...], g_ref[...], sh_ref[...], sc_ref[...])
    fn_ref[...] = fn

    f_hi, f_lo = _split_bf16(fn)
    w_hi, w_lo = _split_bf16(rw_ref[...])
    logits = _dot_nt(w_hi, f_hi) + _dot_nt(w_lo, f_hi) + _dot_nt(w_hi, f_lo) + rb_ref[...]

    e_iota = lax.broadcasted_iota(jnp.int32, (n_e, tm), 0).astype(F32)
    taken = jnp.zeros((n_e, tm), F32)
    neg_inf = jnp.float32(-jnp.inf)
    sels, vals, idxs = [], [], []
    for _ in range(TOP_K):
        masked = jnp.where(taken > 0.0, neg_inf, logits)
        m = jnp.max(masked, axis=0, keepdims=True)
        cand = jnp.logical_and(masked == m, taken == 0.0)
        idx = jnp.min(jnp.where(cand, e_iota, float(n_e)), axis=0, keepdims=True)
        sel = e_iota == idx
        taken = jnp.where(sel, 1.0, taken)
        sels.append(sel)
        vals.append(m)
        idxs.append(idx)
    exps = [jnp.exp(v - vals[0]) for v in vals]
    denom = exps[0] + exps[1] + exps[2] + exps[3]
    onehot = jnp.zeros((n_e, tm), F32)
    for sel in sels:
        onehot = onehot + jnp.where(sel, 1.0, 0.0)

    r_iota = lax.broadcasted_iota(jnp.int32, (tm, tm), 0)
    c_iota = lax.broadcasted_iota(jnp.int32, (tm, tm), 1)
    upper = jnp.where(r_iota < c_iota, 1.0, 0.0).astype(BF16)
    before = _dot(onehot.astype(BF16), upper) + carry_ref[:, 0:1]
    for k in range(TOP_K):
        idx_ref[pl.ds(k, 1), :] = idxs[k].astype(jnp.int32)
        prob_ref[pl.ds(k, 1), :] = exps[k] / denom
        rank_ref[pl.ds(k, 1), :] = jnp.sum(jnp.where(sels[k], before, 0.0), axis=0,
                                           keepdims=True).astype(jnp.int32)
    carry_ref[...] = carry_ref[...] + jnp.sum(onehot, axis=1, keepdims=True)
    cnt_ref[...] = carry_ref[...]


def _router(h, g, mods, sh_index, sc_index, rw_t, rb, n_tiles, tile_off):
    t, d = h.shape
    n_e = rw_t.shape[0]
    tm = ROW_TILE
    n_tok = n_tiles * tm
    small = lambda dt: jax.ShapeDtypeStruct((TOP_K, n_tok), dt)
    small_spec = pl.BlockSpec((TOP_K, tm), lambda i: (0, i))
    return pl.pallas_call(
        _router_kernel,
        grid=(n_tiles,),
        in_specs=[
            pl.BlockSpec((tm, d), lambda i: (i + tile_off, 0)),
            pl.BlockSpec((1, d), lambda i: (0, 0)),
            _mod_spec(d, sh_index),
            _mod_spec(d, sc_index),
            pl.BlockSpec((n_e, d), lambda i: (0, 0)),
            pl.BlockSpec((n_e, 1), lambda i: (0, 0)),
        ],
        out_specs=[
            pl.BlockSpec((tm, d), lambda i: (i, 0)),
            small_spec, small_spec, small_spec,
            pl.BlockSpec((n_e, LANES), lambda i: (0, 0)),
        ],
        out_shape=[
            jax.ShapeDtypeStruct((n_tok, d), F32),
            small(jnp.int32), small(F32), small(jnp.int32),
            jax.ShapeDtypeStruct((n_e, LANES), F32),
        ],
        scratch_shapes=[pltpu.VMEM((n_e, LANES), F32)],
        compiler_params=_cparams("arbitrary"),
    )(h, g.reshape(1, d), mods, mods, rw_t, rb.reshape(n_e, 1))


def _dispatch_kernel(gstart_ref, vend_ref, gend_ref, idx_ref, rank_ref, x_ref, o_ref, zero_ref, sem, zsem):
    tm = x_ref.shape[0]
    n_e = vend_ref.shape[0]

    @pl.when(pl.program_id(0) == 0)
    def _():
        zero_ref[...] = jnp.zeros_like(zero_ref)

        def tail_copy(j):
            return pltpu.make_async_copy(zero_ref, o_ref.at[pl.ds(pl.multiple_of(j * tm, tm), tm)], zsem)

        def tail_start(j, c):
            tail_copy(j).start()
            return c

        def tail_wait(j, c):
            tail_copy(j).wait()
            return c

        first_tail = gend_ref[n_e - 1] // tm
        lax.fori_loop(first_tail, o_ref.shape[0] // tm, tail_start, 0)
        lax.fori_loop(first_tail, o_ref.shape[0] // tm, tail_wait, 0)

        def row_copy(r):
            return pltpu.make_async_copy(zero_ref.at[pl.ds(0, 1)], o_ref.at[pl.ds(r, 1)], zsem)

        def block_copy(b):
            return pltpu.make_async_copy(zero_ref.at[pl.ds(0, SUBLANES)],
                                         o_ref.at[pl.ds(pl.multiple_of(b * SUBLANES, SUBLANES), SUBLANES)], zsem)

        def looped(copy_of, method):
            def body(j, c):
                getattr(copy_of(j), method)()
                return c
            return body

        def clear(e, c):
            lo, hi = vend_ref[e], gend_ref[e]
            mid = jnp.minimum((lo + SUBLANES - 1) // SUBLANES * SUBLANES, hi)
            for method in ("start", "wait"):
                lax.fori_loop(lo, mid, looped(row_copy, method), 0)
                lax.fori_loop(mid // SUBLANES, hi // SUBLANES, looped(block_copy, method), 0)
            return c

        lax.fori_loop(0, n_e, clear, 0)

    def copy(r, k):
        dst = gstart_ref[idx_ref[0, k, r]] + rank_ref[0, k, r]
        return pltpu.make_async_copy(x_ref.at[pl.ds(r, 1)], o_ref.at[pl.ds(dst, 1)], sem)

    def start(r, c):
        for k in range(TOP_K):
            copy(r, k).start()
        return c

    def wait(r, c):
        for k in range(TOP_K):
            copy(r, k).wait()
        return c

    lax.fori_loop(0, tm, start, 0)
    lax.fori_loop(0, tm, wait, 0)


def _route_tiles(a, rows):
    return a.reshape(TOP_K, a.shape[1] // rows, rows).transpose(1, 0, 2)


def _dispatch(fn, idx, rank, group_start, valid_end, group_end, n_sorted):
    t, d = fn.shape
    tm = ROW_TILE
    route_spec = pl.BlockSpec((1, TOP_K, tm), lambda i, *_: (i, 0, 0), memory_space=pltpu.SMEM)
    grid_spec = pltpu.PrefetchScalarGridSpec(
        num_scalar_prefetch=3,
        grid=(t // tm,),
        in_specs=[
            route_spec,
            route_spec,
            pl.BlockSpec((tm, d), lambda i, *_: (i, 0)),
        ],
        out_specs=pl.BlockSpec(memory_space=pl.ANY),
        scratch_shapes=[pltpu.VMEM((tm, d), F32), pltpu.SemaphoreType.DMA(()), pltpu.SemaphoreType.DMA(())],
    )
    return pl.pallas_call(
        _dispatch_kernel,
        grid_spec=grid_spec,
        out_shape=jax.ShapeDtypeStruct((n_sorted, d), F32),
        compiler_params=_cparams("arbitrary"),
    )(group_start, valid_end, group_end, _route_tiles(idx, tm), _route_tiles(rank, tm), fn)


def _combine_kernel(gstart_ref, idx_ref, rank_ref, y_ref, p_ref, h_ref, gate_ref, fg_ref, o_ref, buf_ref, sem,
                    *, final_norm):
    tm = h_ref.shape[0]

    def copy(r, k):
        src = gstart_ref[idx_ref[0, k, r]] + rank_ref[0, k, r]
        return pltpu.make_async_copy(y_ref.at[pl.ds(src, 1)], buf_ref.at[k, pl.ds(r, 1)], sem)

    def start(r, c):
        for k in range(TOP_K):
            copy(r, k).start()
        return c

    def wait(r, c):
        for k in range(TOP_K):
            copy(r, k).wait()
        return c

    lax.fori_loop(0, tm, start, 0)
    lax.fori_loop(0, tm, wait, 0)
    p = p_ref[...]
    acc = p[:, 0:1] * buf_ref[0]
    for k in range(1, TOP_K):
        acc = acc + p[:, k:k + 1] * buf_ref[k]
    out = h_ref[...] + gate_ref[...] * acc
    if final_norm:
        ms = jnp.mean(out * out, axis=-1, keepdims=True)
        out = out * lax.rsqrt(ms + EPS) * fg_ref[...]
    o_ref[...] = out


def _combine(y, idx, rank, group_start, probs, h, mods, gate_index, final_g, final_norm, tm=128):
    n_tok = probs.shape[0]
    d = y.shape[1]
    route_spec = pl.BlockSpec((1, TOP_K, tm), lambda i, *_: (i, 0, 0), memory_space=pltpu.SMEM)
    grid_spec = pltpu.PrefetchScalarGridSpec(
        num_scalar_prefetch=1,
        grid=(n_tok // tm,),
        in_specs=[
            route_spec,
            route_spec,
            pl.BlockSpec(memory_space=pl.ANY),
            pl.BlockSpec((tm, TOP_K), lambda i, *_: (i, 0)),
            pl.BlockSpec((tm, d), lambda i, *_: (i, 0)),
            _mod_spec(d, gate_index),
            pl.BlockSpec((1, d), lambda i, *_: (0, 0)),
        ],
        out_specs=pl.BlockSpec((tm, d), lambda i, *_: (i, 0)),
        scratch_shapes=[pltpu.VMEM((TOP_K, tm, d), F32), pltpu.SemaphoreType.DMA(())],
    )
    return pl.pallas_call(
        functools.partial(_combine_kernel, final_norm=final_norm),
        grid_spec=grid_spec,
        out_shape=jax.ShapeDtypeStruct((n_tok, d), F32),
        compiler_params=_cparams("arbitrary"),
    )(group_start, _route_tiles(idx, tm), _route_tiles(rank, tm), y, probs, h, mods, final_g.reshape(1, d))


W_CHUNK = 512
W_RING = 3


def _experts_kernel(te_ref, tv_ref, xt_ref, slot_ref, nxt_ref, c0_ref, c1_ref,
                    x_ref, b1_ref, b2_ref, w1_hbm, w2_hbm, o_ref,
                    wbf_ref, ring_ref, act_ref, sems, *, layer, n1, n_chunks):
    del xt_ref
    i = pl.program_id(0)
    e_cur = te_ref[i]
    slot = slot_ref[i]
    nxt = nxt_ref[i]
    d = x_ref.shape[1]
    f = act_ref.shape[1]

    def start_chunk(e, c):
        r = lax.rem(c, W_RING)

        @pl.when(c < n1)
        def _():
            pltpu.make_async_copy(w1_hbm.at[layer, e, pl.ds(c * W_CHUNK, W_CHUNK)],
                                  ring_ref.at[r], sems.at[r]).start()

        @pl.when(c >= n1)
        def _():
            pltpu.make_async_copy(w2_hbm.at[layer, e, pl.ds((c - n1) * W_CHUNK, W_CHUNK)],
                                  ring_ref.at[r], sems.at[r]).start()

    def land_chunks(e, dst_slot, c_lo, c_hi):
        def body(c, carry):
            r = lax.rem(c, W_RING)
            pltpu.make_async_copy(w1_hbm.at[layer, 0, pl.ds(0, W_CHUNK)], ring_ref.at[r], sems.at[r]).wait()
            wbf_ref[dst_slot, pl.ds(pl.multiple_of(c * W_CHUNK, W_CHUNK), W_CHUNK), :] = ring_ref[r].astype(BF16)

            @pl.when(c + W_RING < n_chunks)
            def _():
                start_chunk(e, c + W_RING)
            return carry

        lax.fori_loop(c_lo, c_hi, body, 0)

    @pl.when(i == 0)
    def _():
        for c in range(W_RING):
            start_chunk(e_cur, jnp.int32(c))
        land_chunks(e_cur, slot, 0, n_chunks)

    first = jnp.logical_or(i == 0, e_cur != te_ref[jnp.maximum(i - 1, 0)])

    @pl.when(jnp.logical_and(first, nxt >= 0))
    def _():
        for c in range(W_RING):
            start_chunk(nxt, jnp.int32(c))

    @pl.when(tv_ref[i] > 0)
    def _():
        x = x_ref[...].astype(BF16)
        tn = 512
        for j in range(f // tn):
            ca = slice(j * tn, (j + 1) * tn)
            cb = slice(f + j * tn, f + (j + 1) * tn)
            glu = _dot(x, wbf_ref[slot, 0:d, ca]) + b1_ref[0, :, ca]
            lin = _dot(x, wbf_ref[slot, 0:d, cb]) + b1_ref[0, :, cb]
            glu = jnp.minimum(glu, SWIGLU_LIMIT)
            lin = jnp.clip(lin, -SWIGLU_LIMIT, SWIGLU_LIMIT)
            act_ref[:, ca] = (glu * _sigmoid(SWIGLU_ALPHA * glu) * (lin + 1.0)).astype(BF16)
        o_ref[...] = _dot(act_ref[...], wbf_ref[slot, d:d + f, :]) + b2_ref[0]

    @pl.when(tv_ref[i] == 0)
    def _():
        o_ref[...] = jnp.zeros_like(o_ref)

    @pl.when(nxt >= 0)
    def _():
        land_chunks(nxt, 1 - slot, c0_ref[i], c1_ref[i])


def _experts(xs, sched, w1, b1, w2, b2, layer):
    d = xs.shape[1]
    n_tiles = sched[0].shape[0]
    _, n_e, _, two_f = w1.shape
    f = two_f // 2
    assert d == two_f and d % W_CHUNK == 0 and f % W_CHUNK == 0
    n1, n_chunks = d // W_CHUNK, (d + f) // W_CHUNK
    tm = ROW_TILE
    grid_spec = pltpu.PrefetchScalarGridSpec(
        num_scalar_prefetch=7,
        grid=(n_tiles,),
        in_specs=[
            pl.BlockSpec((tm, d), lambda i, te, tv, xt, *_: (xt[i], 0)),
            pl.BlockSpec((1, 1, two_f), lambda i, te, *_: (layer * n_e + te[i], 0, 0)),
            pl.BlockSpec((1, 1, d), lambda i, te, *_: (layer * n_e + te[i], 0, 0)),
            pl.BlockSpec(memory_space=pl.ANY),
            pl.BlockSpec(memory_space=pl.ANY),
        ],
        out_specs=pl.BlockSpec((tm, d), lambda i, *_: (i, 0)),
        scratch_shapes=[
            pltpu.VMEM((2, d + f, d), BF16),
            pltpu.VMEM((W_RING, W_CHUNK, d), F32),
            pltpu.VMEM((tm, f), BF16),
            pltpu.SemaphoreType.DMA((W_RING,)),
        ],
    )
    return pl.pallas_call(
        functools.partial(_experts_kernel, layer=layer, n1=n1, n_chunks=n_chunks),
        grid_spec=grid_spec,
        out_shape=jax.ShapeDtypeStruct((n_tiles * tm, d), F32),
        compiler_params=_cparams("arbitrary"),
    )(*sched, xs, b1.reshape(-1, 1, two_f), b2.reshape(-1, 1, d), w1, w2)


def _moe(h, g, mods, mod_index, router_w, router_b, w1, b1, w2, b2, layer, final_g, n_tiles, final_norm):
    tm = ROW_TILE
    n_e = router_w.shape[1]
    n_tok = n_tiles * tm
    fn, idx, prob, rank, cnt = _router(h, g, mods, mod_index(3, tm), mod_index(4, tm), router_w.T, router_b,
                                       n_tiles, 0)

    i32 = jnp.int32
    counts = cnt[:, 0].astype(i32)
    e_tiles = (counts + tm - 1) // tm
    e_end = jnp.cumsum(e_tiles)
    e_start = e_end - e_tiles
    experts = jnp.arange(n_e, dtype=i32)
    group_start = e_start * tm
    n_tiles_sorted = (n_tok * TOP_K) // tm + n_e
    tiles = jnp.arange(n_tiles_sorted, dtype=i32)
    tile_valid = (tiles < e_end[-1]).astype(i32)
    last_tile = jnp.maximum(e_end[-1] - 1, 0)
    tile_expert = jnp.sum((e_end[None, :] <= jnp.minimum(tiles, last_tile)[:, None]).astype(i32), axis=1)
    tile_expert = jnp.minimum(tile_expert, n_e - 1)
    nonempty = e_tiles > 0
    later = jnp.where(nonempty[None, :] & (experts[None, :] > experts[:, None]), experts[None, :], n_e)
    e_next = jnp.min(later, axis=1)
    e_next = jnp.where(e_next < n_e, e_next, -1)
    e_slot = (jnp.cumsum(nonempty.astype(i32)) - nonempty.astype(i32)) % 2
    n_chunks = (w1.shape[2] + w2.shape[2]) // W_CHUNK
    local = tiles - e_start[tile_expert]
    per = jnp.maximum(e_tiles[tile_expert], 1)
    tile_next = jnp.where(tile_valid > 0, e_next[tile_expert], -1)
    has_next = tile_next >= 0
    c_lo = jnp.where(has_next, (n_chunks * local) // per, 0)
    c_hi = jnp.where(has_next, (n_chunks * (local + 1)) // per, 0)
    x_tile = jnp.minimum(tiles, last_tile)
    sched = (tile_expert, tile_valid, x_tile, e_slot[tile_expert], tile_next, c_lo, c_hi)

    xs = _dispatch(fn, idx, rank, group_start, group_start + counts, e_end * tm, n_tiles_sorted * tm)
    y = _experts(xs, sched, w1, b1, w2, b2, layer)
    tmc = 128
    return _combine(y, idx, rank, group_start, prob.T, h, mods, mod_index(5, tmc), final_g, final_norm, tm=tmc)


def kernel(x, c, ctx, c_ctx, ada_w, ada_b, norm1_g, norm2_g, conv_pw1_w, conv_pw1_b, conv_dw_w, conv_dw_b, conv_ln_g, conv_ln_b, conv_pw2_w, conv_pw2_b, lru_wy, lru_by, lru_wx, lru_bx, lru_conv_w, lru_conv_b, lru_wa, lru_ba, lru_wi, lru_bi, lru_lambda, lru_wo, lru_bo, router_w, router_b, moe_w1, moe_b1, moe_w2, moe_b2, final_g):
    n_batch, seq, d = x.shape
    ctx_len = ctx.shape[1]
    depth = ada_w.shape[0]
    tm = ROW_TILE
    n_ctx, n_lat = n_batch * ctx_len, n_batch * seq
    t_all = n_ctx + n_lat
    lat_tiles, all_tiles = n_lat // tm, t_all // tm
    ctx_src = n_batch
    n_src = n_batch + 1
    big = 512
    assert depth == 2 and ctx_len == tm and seq % big == 0 and n_ctx % big == 0 and n_src <= SUBLANES

    src = jnp.zeros((SUBLANES, d), F32).at[:n_batch].set(c).at[ctx_src].set(c_ctx)
    mods = _ada(src, ada_w, ada_b)[:, :n_src].reshape(depth * n_src * N_ADA, 1, d)

    def mod_index(layer):
        def for_chunk(chunk, rows):
            def index(i):
                row = i * rows
                tile_src = jnp.where(row >= n_lat, ctx_src, row // seq)
                return (layer * n_src + tile_src) * N_ADA + chunk
            return index
        return for_chunk

    bf = lambda a: a.astype(BF16)
    x_rows, ctx_rows = x.reshape(n_lat, d), ctx.reshape(n_ctx, d)

    mi = mod_index(0)
    xn = _norm_mod2(x_rows, ctx_rows, norm1_g[0], mods, mi(0, tm), mi(1, tm))
    v = _mm_glu(xn, bf(conv_pw1_w[0]), conv_pw1_b[0], tm=tm, tn=d)
    half = d // 2
    dw_w, dw_b = conv_dw_w[0], conv_dw_b[0]
    cv = jnp.zeros((t_all, d), F32)
    cv = _conv_seg(v, dw_w, dw_b, cv, seg=ctx_len, row0=n_lat, n_rows=n_ctx, ch0=0, n_ch=d, rows_blk=ctx_len)
    cv = _conv_seg(v, dw_w, dw_b, cv, seg=GRID_W, row0=0, n_rows=n_lat, ch0=0, n_ch=half)
    cv = _conv_col(v, dw_w, dw_b, cv, stride=GRID_W, row0=0, n_col_rows=seq, n_cols=n_batch, ch0=half, n_ch=half)
    h = _mm_ln_res(cv, conv_ln_g[0], conv_ln_b[0], bf(conv_pw2_w[0]), conv_pw2_b[0], x_rows, ctx_rows, mods,
                   mi(2, tm))
    h = _moe(h, norm2_g[0], mods, mi, router_w[0], router_b[0], moe_w1, moe_b1, moe_w2, moe_b2, 0,
             final_g, all_tiles, False)

    mi = mod_index(1)
    xn = _norm_mod(h, norm1_g[1], mods, mi(0, tm), mi(1, tm), all_tiles)
    p, yl = _mm_pair(xn, bf(lru_wx[0]), lru_bx[0], bf(lru_wy[0]), lru_by[0], tm=tm, tn=d)
    common = dict(n_batch=n_batch, ctx_len=ctx_len, seq=seq)
    hf = _lru_direction(p, lru_conv_w[0], lru_conv_b[0], bf(lru_wa[0, 0]), lru_ba[0, 0], bf(lru_wi[0, 0]),
                        lru_bi[0, 0], lru_lambda[0, 0], reverse=False, **common)
    z = _lru_direction(p, lru_conv_w[0], lru_conv_b[0], bf(lru_wa[0, 1]), lru_ba[0, 1], bf(lru_wi[0, 1]),
                       lru_bi[0, 1], lru_lambda[0, 1], reverse=True, h_in=hf, y=yl, **common)
    h_lat = _mm_res(z, bf(lru_wo[0]), lru_bo[0], h, mods, mi(2, big), 0, tm=big, tn=d)
    out = _moe(h_lat, norm2_g[1], mods, mi, router_w[1], router_b[1], moe_w1, moe_b1, moe_w2, moe_b2, 1,
               final_g, lat_tiles, True)
    return out.reshape(n_batch, seq, d)
```
